```python
import math
import jax, jax.numpy as jnp
from jax import lax
import numpy as np

D_MODEL = 1024
BATCH = 4
SEQ = 4096
DEPTH = 4

N_EVEN = (DEPTH + 1) // 2
N_ODD = DEPTH // 2
EPS = 1e-6

W_A = 1024
S5_GROUP = 16
S5_GROUPS = W_A // S5_GROUP
S5_STATE = 64
S5_DT_MIN = 1e-3
S5_DT_MAX = 1e-1
W_B = 1024
CONV_K = 31
EV_INNER = W_A + W_B
EV_IN = 2 * W_A + 3 * W_B

GLA_HEADS = 4
GLA_DV = 2048
GLA_DK = GLA_DV // 2
GLA_HK = GLA_DK // GLA_HEADS
GLA_HV = GLA_DV // GLA_HEADS
GLA_LOWRANK = 16
GLA_TAU = 16.0
GLA_CHUNK = 64
OD_IN = 2 * GLA_DK + 2 * GLA_DV + GLA_LOWRANK

kernel_name = "hybrid_s5_conformer_gla_sandwich"


def rms_norm(x, g):
    xf = x.astype(jnp.float32)
    y = xf * lax.rsqrt(jnp.mean(xf * xf, axis=-1, keepdims=True) + EPS)
    return (y * g.astype(jnp.float32)).astype(x.dtype)


def layer_norm(x, g, b):
    xf = x.astype(jnp.float32)
    mu = jnp.mean(xf, axis=-1, keepdims=True)
    xc = xf - mu
    y = xc * lax.rsqrt(jnp.mean(xc * xc, axis=-1, keepdims=True) + EPS)
    return (y * g.astype(jnp.float32) + b.astype(jnp.float32)).astype(x.dtype)


def s5_layer(u, lam_re, lam_im, log_dt, b_re, b_im, c_re, c_im, d):
    f32 = jnp.float32
    bsz, seq, _ = u.shape
    uf = u.astype(f32).reshape(bsz, seq, S5_GROUPS, S5_GROUP)
    lam = lax.complex(lam_re.astype(f32), lam_im.astype(f32))
    dt = jnp.exp(log_dt.astype(f32))[:, None]
    lam_bar = jnp.exp(lam * dt)
    b_mat = lax.complex(b_re.astype(f32), b_im.astype(f32))
    b_bar = ((lam_bar - 1.0) / lam)[..., None] * b_mat
    bu = jnp.einsum('gnp,blgp->blgn', b_bar, uf)
    a = jnp.broadcast_to(lam_bar, (1, seq) + lam_bar.shape)

    def combine(e1, e2):
        a1, b1 = e1
        a2, b2 = e2
        return a2 * a1, a2 * b1 + b2

    _, states = lax.associative_scan(combine, (a, bu), axis=1)
    c_mat = lax.complex(c_re.astype(f32), c_im.astype(f32))
    y = jnp.einsum('gpn,blgn->blgp', c_mat, states).real
    y = y + d.astype(f32).reshape(S5_GROUPS, S5_GROUP) * uf
    return y.reshape(bsz, seq, W_A).astype(u.dtype)


def conformer_conv(val, gate, w_dw, b_dw, ln_g, ln_b, w_pw, b_pw):
    h = val * jax.nn.sigmoid(gate)
    h = lax.conv_general_dilated(
        h, w_dw[:, None, :].astype(h.dtype), window_strides=(1,),
        padding=((CONV_K - 1, 0),), dimension_numbers=('NWC', 'WIO', 'NWC'),
        feature_group_count=W_B) + b_dw
    h = jax.nn.silu(layer_norm(h, ln_g, ln_b))
    return h @ w_pw + b_pw


def gla_chunked(q, k, v, g):
    bsz, seq, nh, dk = q.shape
    dv = v.shape[-1]
    nc = seq // GLA_CHUNK

    def chunks(t):
        return t.reshape(bsz, nc, GLA_CHUNK, nh, t.shape[-1]).transpose(1, 0, 3, 2, 4)

    qc, kc, vc = chunks(q), chunks(k), chunks(v)
    bc = jnp.cumsum(chunks(g), axis=3)
    causal = jnp.tril(jnp.ones((GLA_CHUNK, GLA_CHUNK), dtype=bool))

    def step(state, inp):
        qi, ki, vi, bi = inp
        b_last = bi[:, :, -1:, :]
        q_dec = qi * jnp.exp(bi)
        attn = jnp.einsum('bhik,bhjk->bhij', q_dec, ki * jnp.exp(-bi))
        attn = jnp.where(causal, attn, 0.0)
        o = (jnp.einsum('bhij,bhjv->bhiv', attn, vi)
             + jnp.einsum('bhik,bhkv->bhiv', q_dec, state))
        state = (jnp.exp(b_last[:, :, 0, :, None]) * state
                 + jnp.einsum('bhjk,bhjv->bhkv', ki * jnp.exp(b_last - bi), vi))
        return state, o

    s0 = jnp.zeros((bsz, nh, dk, dv), jnp.float32)
    _, o = lax.scan(step, s0, (qc, kc, vc, bc))
    return o.transpose(1, 0, 3, 2, 4).reshape(bsz, seq, nh, dv)


def gla_branch(p, w_gate_up, b_gate, norm_g):
    f32 = jnp.float32
    bsz, seq, _ = p.shape
    q, k, v, r, lr = jnp.split(
        p, [GLA_DK, 2 * GLA_DK, 2 * GLA_DK + GLA_DV, 2 * GLA_DK + 2 * GLA_DV], axis=-1)
    g = jax.nn.log_sigmoid((lr @ w_gate_up + b_gate).astype(f32)) / GLA_TAU

    def heads(t, dh):
        return t.astype(f32).reshape(bsz, seq, GLA_HEADS, dh)

    o = gla_chunked(heads(q, GLA_HK) * (GLA_HK ** -0.5), heads(k, GLA_HK),
                    heads(v, GLA_HV), heads(g, GLA_HK))
    o = rms_norm(o, norm_g)
    return o.reshape(bsz, seq, GLA_DV).astype(p.dtype) * jax.nn.silu(r)


def setup_inputs(seed: int = 0) -> dict:
    key = jax.random.key(seed)
    ks = jax.random.split(key, 28)
    f32 = jnp.float32

    def nrm(k, shape, scale):
        return jax.random.normal(k, shape, f32) * scale

    NE, NO = N_EVEN, N_ODD
    n_idx = jnp.arange(S5_STATE, dtype=f32)
    return {
        "x": nrm(ks[0], (BATCH, SEQ, D_MODEL), 1.0),
        "norm_pre": 1.0 + nrm(ks[1], (DEPTH, D_MODEL), 0.05),
        "norm_post": 1.0 + nrm(ks[2], (DEPTH, D_MODEL), 0.05),
        "ev_w_in": nrm(ks[3], (NE, D_MODEL, EV_IN), D_MODEL ** -0.5),
        "s5_lambda_re": -0.5 + nrm(ks[4], (NE, S5_GROUPS, S5_STATE), 0.01),
        "s5_lambda_im": jnp.pi * n_idx + nrm(ks[5], (NE, S5_GROUPS, S5_STATE), 0.01),
        "s5_log_dt": jax.random.uniform(ks[6], (NE, S5_GROUPS), f32,
                                        minval=math.log(S5_DT_MIN), maxval=math.log(S5_DT_MAX)),
        "s5_b_re": nrm(ks[7], (NE, S5_GROUPS, S5_STATE, S5_GROUP), (2 * S5_GROUP) ** -0.5),
        "s5_b_im": nrm(ks[8], (NE, S5_GROUPS, S5_STATE, S5_GROUP), (2 * S5_GROUP) ** -0.5),
        "s5_c_re": nrm(ks[9], (NE, S5_GROUPS, S5_GROUP, S5_STATE), S5_STATE ** -0.5),
        "s5_c_im": nrm(ks[10], (NE, S5_GROUPS, S5_GROUP, S5_STATE), S5_STATE ** -0.5),
        "s5_d": nrm(ks[11], (NE, W_A), 1.0),
        "s5_w_glu": nrm(ks[12], (NE, W_A, W_A), W_A ** -0.5),
        "s5_b_glu": nrm(ks[13], (NE, W_A), 0.02),
        "conv_w": nrm(ks[14], (NE, CONV_K, W_B), CONV_K ** -0.5),
        "conv_b": nrm(ks[15], (NE, W_B), 0.02),
        "conv_ln_g": 1.0 + nrm(ks[16], (NE, W_B), 0.05),
        "conv_ln_b": nrm(ks[17], (NE, W_B), 0.02),
        "conv_w_pw": nrm(ks[18], (NE, W_B, W_B), W_B ** -0.5),
        "conv_b_pw": nrm(ks[19], (NE, W_B), 0.02),
        "ev_w_out": nrm(ks[20], (NE, EV_INNER, D_MODEL), EV_INNER ** -0.5),
        "od_w_in": nrm(ks[21], (NO, D_MODEL, OD_IN), D_MODEL ** -0.5),
        "gla_w_gate_up": nrm(ks[22], (NO, GLA_LOWRANK, GLA_DK), GLA_LOWRANK ** -0.5),
        "gla_b_gate": nrm(ks[23], (NO, GLA_DK), 0.1),
        "gla_norm_g": 1.0 + nrm(ks[24], (NO, GLA_HV), 0.05),
        "od_w_out": nrm(ks[25], (NO, GLA_DV, D_MODEL), GLA_DV ** -0.5),
    }


def reference(x, norm_pre, norm_post, ev_w_in, s5_lambda_re, s5_lambda_im, s5_log_dt,
              s5_b_re, s5_b_im, s5_c_re, s5_c_im, s5_d, s5_w_glu, s5_b_glu,
              conv_w, conv_b, conv_ln_g, conv_ln_b, conv_w_pw, conv_b_pw, ev_w_out,
              od_w_in, gla_w_gate_up, gla_b_gate, gla_norm_g, od_w_out):
    for i in range(DEPTH):
        u = rms_norm(x, norm_pre[i])
        if i % 2 == 0:
            j = i // 2
            p = u @ ev_w_in[j]
            a_in, a_z, b_val, b_gate, b_z = jnp.split(
                p, [W_A, 2 * W_A, 2 * W_A + W_B, 2 * W_A + 2 * W_B], axis=-1)
            ya = s5_layer(a_in, s5_lambda_re[j], s5_lambda_im[j], s5_log_dt[j],
                          s5_b_re[j], s5_b_im[j], s5_c_re[j], s5_c_im[j], s5_d[j])
            ya = jax.nn.gelu(ya)
            ya = ya * jax.nn.sigmoid(ya @ s5_w_glu[j] + s5_b_glu[j])
            ya = ya * jax.nn.silu(a_z)
            yb = conformer_conv(b_val, b_gate, conv_w[j], conv_b[j], conv_ln_g[j],
                                conv_ln_b[j], conv_w_pw[j], conv_b_pw[j])
            yb = yb * jax.nn.silu(b_z)
            y = jnp.concatenate([ya, yb], axis=-1) @ ev_w_out[j]
        else:
            j = i // 2
            p = u @ od_w_in[j]
            y = gla_branch(p, gla_w_gate_up[j], gla_b_gate[j], gla_norm_g[j]) @ od_w_out[j]
        x = x + rms_norm(y, norm_post[i])
    return x
```

```python
import functools
import math

import jax
import jax.numpy as jnp
from jax import lax
from jax.experimental import pallas as pl
from jax.experimental.pallas import tpu as pltpu

F32 = jnp.float32
BF16 = jnp.bfloat16

EPS = 1e-6
LANES = 128
SUBLANES = 8

S5_GROUP = 16
S5_STATE = 64
S5_T = 16
S5_GPB = LANES // S5_GROUP
S5_SB = S5_GPB * S5_STATE
CONV_K = 31
CONV_HALO = 32
GLA_HEADS = 4
GLA_HK = 256
GLA_HV = 512
GLA_LOWRANK = 16
GLA_TAU = 16.0
GLA_CHUNK = 64

NT_DIMS = (((1,), (1,)), ((), ()))
TN_DIMS = (((0,), (0,)), ((), ()))


def _cparams(sem, vmem_mb):
    return pltpu.CompilerParams(dimension_semantics=sem, vmem_limit_bytes=vmem_mb * 1024 * 1024)


def _const_spec(shape):
    nd = len(shape)
    return pl.BlockSpec(shape, lambda *_: (0,) * nd, pipeline_mode=pl.Buffered(1))


def _rms_scale(x):
    return lax.rsqrt(jnp.mean(x * x, axis=-1, keepdims=True) + EPS)


def _sigmoid(x):
    return 1.0 / (1.0 + jnp.exp(-x))


def _silu(x):
    return x * _sigmoid(x)


def _norm_proj_kernel(x_ref, g_ref, *refs, n_chunk):
    n_w = len(refs) // 2
    w_refs, o_refs = refs[:n_w], refs[n_w:]
    x = x_ref[...]
    u = (x * _rms_scale(x) * g_ref[...]).astype(BF16)
    for w_ref, o_ref in zip(w_refs, o_refs):
        n = w_ref.shape[1]
        for n0 in range(0, n, n_chunk):
            n1 = min(n0 + n_chunk, n)
            o_ref[:, n0:n1] = jnp.dot(u, w_ref[:, n0:n1],
                                      preferred_element_type=F32).astype(o_ref.dtype)


def _norm_proj(x2, g, ws, out_dtypes, tm=512, n_chunk=512):
    m, d = x2.shape
    in_specs = [pl.BlockSpec((tm, d), lambda i: (i, 0)), _const_spec((1, d))]
    in_specs += [_const_spec(w.shape) for w in ws]
    out_specs = [pl.BlockSpec((tm, w.shape[1]), lambda i: (i, 0)) for w in ws]
    out_shape = [jax.ShapeDtypeStruct((m, w.shape[1]), dt) for w, dt in zip(ws, out_dtypes)]
    return pl.pallas_call(
        functools.partial(_norm_proj_kernel, n_chunk=n_chunk),
        grid=(m // tm,), in_specs=in_specs, out_specs=out_specs, out_shape=out_shape,
        compiler_params=_cparams(("parallel",), 56), name="norm_proj",
    )(x2, g.reshape(1, d), *ws)


def _s5_prep_kernel(lre_ref, lim_ref, dt_ref, btr_ref, bti_ref, cr_ref, ci_ref,
                    g_ref, wz_ref, vt_ref, lp_ref):
    lre, lim, dt = lre_ref[...], lim_ref[...], jnp.exp(dt_ref[...])

    def lam_pow(m):
        mag = jnp.exp(lre * dt * float(m))
        ang = lim * dt * float(m)
        return mag * jnp.cos(ang), mag * jnp.sin(ang)

    pows = [lam_pow(m) for m in range(S5_T + 1)]
    nr, ni = pows[1][0] - 1.0, pows[1][1]
    den = lre * lre + lim * lim
    cfr = (nr * lre + ni * lim) / den
    cfi = (ni * lre - nr * lim) / den
    btr, bti = btr_ref[...], bti_ref[...]
    bbr = cfr * btr - cfi * bti
    bbi = cfr * bti + cfi * btr
    cr, ci = cr_ref[...], ci_ref[...]

    for j in range(S5_T):
        pr, pi = pows[S5_T - 1 - j]
        wz_ref[j * LANES:(j + 1) * LANES, 0:S5_SB] = (pr * bbr - pi * bbi).astype(wz_ref.dtype)
        wz_ref[j * LANES:(j + 1) * LANES, S5_SB:] = (pr * bbi + pi * bbr).astype(wz_ref.dtype)
    for i in range(S5_T):
        pr, pi = pows[i + 1]
        vt_ref[i * LANES:(i + 1) * LANES, 0:S5_SB] = (cr * pr - ci * pi).astype(vt_ref.dtype)
        vt_ref[i * LANES:(i + 1) * LANES, S5_SB:] = (-(cr * pi + ci * pr)).astype(vt_ref.dtype)
    bb = jnp.concatenate([bbr, bbi], axis=1)
    kts = []
    for tau in range(S5_T):
        pr, pi = pows[tau]
        ct = jnp.concatenate([cr * pr - ci * pi, -(cr * pi + ci * pr)], axis=1)
        kts.append(lax.dot_general(bb, ct, NT_DIMS, precision=lax.Precision.HIGHEST,
                                   preferred_element_type=F32))
    zero = jnp.zeros((LANES, LANES), F32)
    for r in range(S5_T):
        for e in range(2):
            tau = S5_T - 2 - r + e
            kt = kts[tau] if tau >= 0 else zero
            g_ref[r * LANES:(r + 1) * LANES, e * LANES:(e + 1) * LANES] = kt.astype(g_ref.dtype)
    for r in range(2 * SUBLANES):
        pr, pi = lam_pow(S5_T * r) if r <= SUBLANES else (jnp.zeros_like(lre), jnp.zeros_like(lre))
        lp_ref[r:r + 1, 0:S5_SB] = pr
        lp_ref[r:r + 1, S5_SB:] = pi


def _s5_prep(lam_re, lam_im, log_dt, b_re, b_im, c_re, c_im):
    ng = lam_re.shape[0]
    nblk = ng // S5_GPB
    eye = jnp.eye(S5_GPB, dtype=F32)

    def rows(t):
        return t.astype(F32).reshape(nblk, 1, S5_SB)

    def embed(t):
        t = t.astype(F32).reshape(nblk, S5_GPB, S5_GROUP, S5_STATE)
        t = t[:, :, :, None, :] * eye[None, :, None, :, None]
        return t.reshape(nblk, LANES, S5_SB)

    dt_rows = jnp.broadcast_to(log_dt.astype(F32)[:, None], (ng, S5_STATE))
    ins = [rows(lam_re), rows(lam_im), rows(dt_rows),
           embed(jnp.swapaxes(b_re, 1, 2)), embed(jnp.swapaxes(b_im, 1, 2)),
           embed(c_re), embed(c_im)]
    row_spec = pl.BlockSpec((None, 1, S5_SB), lambda k: (k, 0, 0))
    mat_spec = pl.BlockSpec((None, LANES, S5_SB), lambda k: (k, 0, 0))
    kt = S5_T * LANES
    out_shape = [jax.ShapeDtypeStruct((nblk, kt, 2 * LANES), BF16),
                 jax.ShapeDtypeStruct((nblk, kt, 2 * S5_SB), BF16),
                 jax.ShapeDtypeStruct((nblk, kt, 2 * S5_SB), BF16),
                 jax.ShapeDtypeStruct((nblk, 2 * SUBLANES, 2 * S5_SB), F32)]
    out_specs = [pl.BlockSpec((None,) + s.shape[1:], lambda k: (k, 0, 0)) for s in out_shape]
    return pl.pallas_call(
        _s5_prep_kernel, grid=(nblk,),
        in_specs=[row_spec] * 3 + [mat_spec] * 4, out_specs=out_specs, out_shape=out_shape,
        compiler_params=_cparams(("parallel",), 48), name="s5_prep",
    )(*ins)


def _s5_kernel(u_ref, g_ref, wz_ref, vt_ref, lp_ref, d_ref, o_ref, u16_ref, z_ref, s_ref):
    nc = u16_ref.shape[0]
    for j in range(S5_T):
        u16_ref[:, j * LANES:(j + 1) * LANES] = u_ref[pl.ds(j, nc, stride=S5_T), :].astype(BF16)
    z_ref[...] = jnp.dot(u16_ref[...], wz_ref[...], preferred_element_type=F32)

    row = lax.broadcasted_iota(jnp.int32, (SUBLANES, S5_SB), 0)

    def shift(x, sh):
        return jnp.where(row >= sh, pltpu.roll(x, sh, axis=0), 0.0)

    def lp(r0, r1):
        return lp_ref[r0:r1, 0:S5_SB], lp_ref[r0:r1, S5_SB:]

    def scan_rows(bi, carry):
        cr, ci = carry
        r0 = pl.multiple_of(bi * SUBLANES, SUBLANES)
        xr = z_ref[pl.ds(r0, SUBLANES), 0:S5_SB]
        xi = z_ref[pl.ds(r0, SUBLANES), S5_SB:]
        for sh in (1, 2, 4):
            pr, pi = lp(sh, sh + 1)
            sr, si = shift(xr, sh), shift(xi, sh)
            xr, xi = xr + pr * sr - pi * si, xi + pr * si + pi * sr
        pr, pi = lp(0, SUBLANES)
        s_ref[pl.ds(r0, SUBLANES), 0:S5_SB] = pr * cr - pi * ci + shift(xr, 1)
        s_ref[pl.ds(r0, SUBLANES), S5_SB:] = pr * ci + pi * cr + shift(xi, 1)
        pr, pi = lp(SUBLANES, SUBLANES + 1)
        last = SUBLANES - 1
        return (pr * cr - pi * ci + xr[last:last + 1], pr * ci + pi * cr + xi[last:last + 1])

    zero = jnp.zeros((1, S5_SB), F32)
    lax.fori_loop(0, nc // SUBLANES, scan_rows, (zero, zero))

    sb = s_ref[...].astype(BF16)
    d = d_ref[...]
    for ip in range(S5_T // 2):
        kk = (2 * ip + 2) * LANES
        acc = jnp.dot(u16_ref[:, 0:kk], g_ref[(S5_T - 2 - 2 * ip) * LANES:, :],
                      preferred_element_type=F32)
        acc = acc + lax.dot_general(sb, vt_ref[2 * ip * LANES:(2 * ip + 2) * LANES, :], NT_DIMS,
                                    preferred_element_type=F32)
        for e in range(2):
            i = 2 * ip + e
            ui = u_ref[pl.ds(i, nc, stride=S5_T), :]
            o_ref[pl.ds(i, nc, stride=S5_T), :] = acc[:, e * LANES:(e + 1) * LANES] + d * ui


def _s5_apply(a_in, gm, wz, vt, lp, d):
    bsz, seq, width = a_in.shape
    nblk = width // LANES
    nc = seq // S5_T
    kt = S5_T * LANES
    act_spec = pl.BlockSpec((None, seq, LANES), lambda k, b: (b, 0, k))

    def wspec(shape):
        return pl.BlockSpec((None,) + shape, lambda k, b: (k, 0, 0))

    return pl.pallas_call(
        _s5_kernel, grid=(nblk, bsz),
        in_specs=[act_spec, wspec((kt, 2 * LANES)), wspec((kt, 2 * S5_SB)), wspec((kt, 2 * S5_SB)),
                  wspec((2 * SUBLANES, 2 * S5_SB)), pl.BlockSpec((1, LANES), lambda k, b: (0, k))],
        out_specs=act_spec,
        out_shape=jax.ShapeDtypeStruct((bsz, seq, width), F32),
        scratch_shapes=[pltpu.VMEM((nc, kt), BF16), pltpu.VMEM((nc, 2 * S5_SB), F32),
                        pltpu.VMEM((nc, 2 * S5_SB), F32)],
        compiler_params=_cparams(("parallel", "parallel"), 56), name="s5_apply",
    )(a_in, gm, wz, vt, lp, d.reshape(1, width))


def _even_tail_kernel(x_ref, ya_ref, az_ref, bv_ref, bg_ref, bz_ref,
                      wglu_ref, bglu_ref, cw_ref, cb_ref, lng_ref, lnb_ref,
                      wpw_ref, bpw_ref, wout_ref, gpost_ref, o_ref, h_ref, c_ref, *, rows):
    tm, w = c_ref.shape

    @pl.when(pl.program_id(1) == 0)
    def _():
        h_ref[0:CONV_HALO, :] = jnp.zeros((CONV_HALO, w), F32)

    h_ref[CONV_HALO:, :] = bv_ref[...].astype(F32) * _sigmoid(bg_ref[...].astype(F32))

    off = CONV_HALO - (CONV_K - 1)

    for r0 in range(0, tm, rows):
        for c0 in range(0, w, LANES):
            acc = jnp.broadcast_to(cb_ref[:, c0:c0 + LANES], (rows, LANES))
            for k in range(CONV_K):
                acc = acc + cw_ref[k:k + 1, c0:c0 + LANES] * h_ref[r0 + off + k:r0 + off + k + rows, c0:c0 + LANES]
            c_ref[r0:r0 + rows, c0:c0 + LANES] = acc
    h_ref[0:CONV_HALO, :] = h_ref[tm:tm + CONV_HALO, :]

    hc = c_ref[...]
    mu = jnp.mean(hc, axis=-1, keepdims=True)
    xc = hc - mu
    hn = xc * lax.rsqrt(jnp.mean(xc * xc, axis=-1, keepdims=True) + EPS) * lng_ref[...] + lnb_ref[...]
    yb = jnp.dot(_silu(hn).astype(BF16), wpw_ref[...], preferred_element_type=F32) + bpw_ref[...]
    yb = yb * _silu(bz_ref[...].astype(F32))

    ya = jax.nn.gelu(ya_ref[...])
    gate = jnp.dot(ya.astype(BF16), wglu_ref[...], preferred_element_type=F32) + bglu_ref[...]
    ya = ya * _sigmoid(gate) * _silu(az_ref[...].astype(F32))

    y = jnp.dot(ya.astype(BF16), wout_ref[0:w, :], preferred_element_type=F32)
    y = y + jnp.dot(yb.astype(BF16), wout_ref[w:, :], preferred_element_type=F32)
    o_ref[...] = x_ref[...] + y * _rms_scale(y) * gpost_ref[...]


def _even_tail(x, ya, p, wglu, bglu, cw, cb, lng, lnb, wpw, bpw, wout, gpost, tm=256, rows=32):
    bsz, seq, d = x.shape
    w = ya.shape[-1]

    def act(col):
        return pl.BlockSpec((None, tm, w), lambda b, t: (b, t, col))

    vec = lambda v: v.reshape(1, -1).astype(F32)
    cw_pad = jnp.zeros((CONV_HALO, w), F32).at[0:CONV_K].set(cw.astype(F32))
    consts = [wglu, vec(bglu), cw_pad, vec(cb), vec(lng), vec(lnb), wpw, vec(bpw), wout, vec(gpost)]
    return pl.pallas_call(
        functools.partial(_even_tail_kernel, rows=rows), grid=(bsz, seq // tm),
        in_specs=[act(0), act(0), act(0), act(1), act(2), act(3)] + [_const_spec(c.shape) for c in consts],
        out_specs=act(0),
        out_shape=jax.ShapeDtypeStruct((bsz, seq, d), F32),
        scratch_shapes=[pltpu.VMEM((CONV_HALO + tm, w), F32), pltpu.VMEM((tm, w), F32)],
        compiler_params=_cparams(("parallel", "arbitrary"), 56), name="even_tail",
    )(x, ya, p, p, p, p, *consts)


def _gla_kernel(q_ref, k_ref, v_ref, r_ref, lr_ref, wg_ref, bg_ref, ng_ref, o_ref, st_ref):
    tb = q_ref.shape[0]
    c = GLA_CHUNK

    @pl.when(pl.program_id(2) == 0)
    def _():
        st_ref[...] = jnp.zeros_like(st_ref)

    gp = jnp.dot(lr_ref[...], wg_ref[...], preferred_element_type=F32) + bg_ref[...]
    g_all = (jnp.minimum(gp, 0.0) - jnp.log1p(jnp.exp(-jnp.abs(gp)))) * (1.0 / GLA_TAU)

    ri = lax.broadcasted_iota(jnp.int32, (c, c), 0)
    ci = lax.broadcasted_iota(jnp.int32, (c, c), 1)
    causal = ri >= ci
    tri = jnp.where(causal, 1.0, 0.0).astype(BF16)

    for n in range(tb // c):
        sl = slice(n * c, (n + 1) * c)
        g = g_all[sl]
        g1 = g.astype(BF16)
        rem = g - g1.astype(F32)
        g2 = rem.astype(BF16)
        g3 = (rem - g2.astype(F32)).astype(BF16)
        bc = (jnp.dot(tri, g1, preferred_element_type=F32) + jnp.dot(tri, g2, preferred_element_type=F32)
              + jnp.dot(tri, g3, preferred_element_type=F32))
        bl = bc[c - 1:c, :]
        q = q_ref[sl, :].astype(F32) * (GLA_HK ** -0.5)
        k = k_ref[sl, :].astype(F32)
        v = v_ref[sl, :]
        qd = (q * jnp.exp(bc)).astype(BF16)
        kd = (k * jnp.exp(-bc)).astype(BF16)
        ku = (k * jnp.exp(bl - bc)).astype(BF16)
        attn = lax.dot_general(qd, kd, NT_DIMS, preferred_element_type=F32)
        attn = jnp.where(causal, attn, 0.0).astype(BF16)
        st = st_ref[...]
        o = jnp.dot(attn, v, preferred_element_type=F32)
        o = o + lax.dot_general(qd, st.astype(BF16), NT_DIMS, preferred_element_type=F32)
        st_ref[...] = st * jnp.exp(bl) + lax.dot_general(v, ku, TN_DIMS, preferred_element_type=F32)
        on = o * _rms_scale(o) * ng_ref[...]
        o_ref[sl, :] = (on * _silu(r_ref[sl, :].astype(F32))).astype(o_ref.dtype)


def _gla(p, wg, bg, ng, tb=256):
    bsz, seq, _ = p.shape
    nh, hk, hv = GLA_HEADS, GLA_HK, GLA_HV
    dk, dv = nh * hk, nh * hv

    def spec(width, base):
        return pl.BlockSpec((None, tb, width), lambda b, h, t: (b, t, base + h))

    lr_col = (2 * dk + 2 * dv) // LANES
    in_specs = [spec(hk, 0), spec(hk, dk // hk), spec(hv, 2 * dk // hv), spec(hv, (2 * dk + dv) // hv),
                pl.BlockSpec((None, tb, LANES), lambda b, h, t: (b, t, lr_col)),
                pl.BlockSpec((LANES, hk), lambda b, h, t: (0, h)),
                pl.BlockSpec((1, hk), lambda b, h, t: (0, h)),
                pl.BlockSpec((1, hv), lambda b, h, t: (0, 0))]
    wg_pad = jnp.zeros((LANES, dk), BF16).at[0:GLA_LOWRANK].set(wg.astype(BF16))
    return pl.pallas_call(
        _gla_kernel, grid=(bsz, nh, seq // tb), in_specs=in_specs,
        out_specs=pl.BlockSpec((None, tb, hv), lambda b, h, t: (b, t, h)),
        out_shape=jax.ShapeDtypeStruct((bsz, seq, dv), BF16),
        scratch_shapes=[pltpu.VMEM((hv, hk), F32)],
        compiler_params=_cparams(("parallel", "parallel", "arbitrary"), 48), name="gla",
    )(p, p, p, p, p, wg_pad, bg.reshape(1, dk).astype(F32), ng.reshape(1, hv).astype(F32))


def _out_proj_kernel(x_ref, y_ref, w_ref, g_ref, o_ref):
    y = jnp.dot(y_ref[...], w_ref[...], preferred_element_type=F32)
    o_ref[...] = x_ref[...] + y * _rms_scale(y) * g_ref[...]


def _out_proj(x2, y2, w, g, tm=512):
    m, d = x2.shape
    kdim = y2.shape[1]
    return pl.pallas_call(
        _out_proj_kernel, grid=(m // tm,),
        in_specs=[pl.BlockSpec((tm, d), lambda i: (i, 0)), pl.BlockSpec((tm, kdim), lambda i: (i, 0)),
                  _const_spec(w.shape), _const_spec((1, d))],
        out_specs=pl.BlockSpec((tm, d), lambda i: (i, 0)),
        out_shape=jax.ShapeDtypeStruct((m, d), F32),
        compiler_params=_cparams(("parallel",), 48), name="out_proj",
    )(x2, y2, w, g.reshape(1, d).astype(F32))


def _even_layer(x, g_pre, g_post, w_in, s5p, s5_d, w_glu, b_glu, cw, cb, lng, lnb, w_pw, b_pw, w_out):
    bsz, seq, d = x.shape
    wa = s5_d.shape[0]
    w_in = w_in.astype(BF16)
    a_in, rest = _norm_proj(x.reshape(bsz * seq, d), g_pre.astype(F32),
                            [w_in[:, :wa], w_in[:, wa:]], [F32, BF16])
    gm, wz, vt, lp = _s5_prep(*s5p)
    ya = _s5_apply(a_in.reshape(bsz, seq, wa), gm, wz, vt, lp, s5_d.astype(F32))
    return _even_tail(x, ya, rest.reshape(bsz, seq, -1), w_glu.astype(BF16), b_glu, cw, cb, lng, lnb,
                      w_pw.astype(BF16), b_pw, w_out.astype(BF16), g_post)


def _odd_layer(x, g_pre, g_post, w_in, wg, bg, ng, w_out):
    bsz, seq, d = x.shape
    n_in = w_in.shape[1]
    n_pad = -(-n_in // LANES) * LANES
    w_pad = jnp.zeros((d, n_pad), BF16).at[:, :n_in].set(w_in.astype(BF16))
    (p,) = _norm_proj(x.reshape(bsz * seq, d), g_pre.astype(F32), [w_pad], [BF16])
    y = _gla(p.reshape(bsz, seq, n_pad), wg, bg, ng)
    out = _out_proj(x.reshape(bsz * seq, d), y.reshape(bsz * seq, -1), w_out.astype(BF16), g_post)
    return out.reshape(bsz, seq, d)


def kernel(x, norm_pre, norm_post, ev_w_in, s5_lambda_re, s5_lambda_im, s5_log_dt, s5_b_re, s5_b_im, s5_c_re, s5_c_im, s5_d, s5_w_glu, s5_b_glu, conv_w, conv_b, conv_ln_g, conv_ln_b, conv_w_pw, conv_b_pw, ev_w_out, od_w_in, gla_w_gate_up, gla_b_gate, gla_norm_g, od_w_out):
    depth = norm_pre.shape[0]
    for i in range(depth):
        j = i // 2
        if i % 2 == 0:
            s5p = (s5_lambda_re[j], s5_lambda_im[j], s5_log_dt[j], s5_b_re[j], s5_b_im[j],
                   s5_c_re[j], s5_c_im[j])
            x = _even_layer(x, norm_pre[i], norm_post[i], ev_w_in[j], s5p, s5_d[j], s5_w_glu[j],
                            s5_b_glu[j], conv_w[j], conv_b[j], conv_ln_g[j], conv_ln_b[j],
                            conv_w_pw[j], conv_b_pw[j], ev_w_out[j])
        else:
            x = _odd_layer(x, norm_pre[i], norm_post[i], od_w_in[j], gla_w_gate_up[j], gla_b_gate[j],
                           gla_norm_g[j], od_w_out[j])
    return x
```

```python
import functools
import math

import jax
import jax.numpy as jnp
from jax import lax
from jax.experimental import pallas as pl
from jax.experimental.pallas import tpu as pltpu

F32 = jnp.float32
BF16 = jnp.bfloat16

EPS = 1e-6
LANES = 128
SUBLANES = 8

S5_GROUP = 16
S5_STATE = 64
S5_T = 16
S5_GPB = LANES // S5_GROUP
S5_SB = S5_GPB * S5_STATE
CONV_K = 31
CONV_HALO = 32
GLA_HEADS = 4
GLA_HK = 256
GLA_HV = 512
GLA_LOWRANK = 16
GLA_TAU = 16.0
GLA_CHUNK = 64

NT_DIMS = (((1,), (1,)), ((), ()))
TN_DIMS = (((0,), (0,)), ((), ()))


def _cparams(sem, vmem_mb):
    return pltpu.CompilerParams(dimension_semantics=sem, vmem_limit_bytes=vmem_mb * 1024 * 1024)


def _const_spec(shape):
    nd = len(shape)
    return pl.BlockSpec(shape, lambda *_: (0,) * nd, pipeline_mode=pl.Buffered(1))


def _rms_scale(x):
    return lax.rsqrt(jnp.mean(x * x, axis=-1, keepdims=True) + EPS)


def _sigmoid(x):
    return 0.5 * jnp.tanh(0.5 * x) + 0.5


def _silu(x):
    return x * _sigmoid(x)


def _norm_proj_kernel(x_ref, g_ref, *refs, n_chunk):
    n_w = len(refs) // 2
    w_refs, o_refs = refs[:n_w], refs[n_w:]
    x = x_ref[...]
    u = (x * _rms_scale(x) * g_ref[...]).astype(BF16)
    for w_ref, o_ref in zip(w_refs, o_refs):
        n = w_ref.shape[1]
        for n0 in range(0, n, n_chunk):
            n1 = min(n0 + n_chunk, n)
            o_ref[:, n0:n1] = jnp.dot(u, w_ref[:, n0:n1],
                                      preferred_element_type=F32).astype(o_ref.dtype)


def _norm_proj(x2, g, ws, out_dtypes, tm=512, n_chunk=512):
    m, d = x2.shape
    in_specs = [pl.BlockSpec((tm, d), lambda i: (i, 0)), _const_spec((1, d))]
    in_specs += [_const_spec(w.shape) for w in ws]
    out_specs = [pl.BlockSpec((tm, w.shape[1]), lambda i: (i, 0)) for w in ws]
    out_shape = [jax.ShapeDtypeStruct((m, w.shape[1]), dt) for w, dt in zip(ws, out_dtypes)]
    return pl.pallas_call(
        functools.partial(_norm_proj_kernel, n_chunk=n_chunk),
        grid=(m // tm,), in_specs=in_specs, out_specs=out_specs, out_shape=out_shape,
        compiler_params=_cparams(("parallel",), 56), name="norm_proj",
    )(x2, g.reshape(1, d), *ws)


def _s5_prep_kernel(lre_ref, lim_ref, dt_ref, btr_ref, bti_ref, cr_ref, ci_ref,
                    g_ref, wz_ref, vt_ref, lp_ref):
    lre, lim, dt = lre_ref[...], lim_ref[...], jnp.exp(dt_ref[...])

    def lam_pow(m):
        mag = jnp.exp(lre * dt * float(m))
        ang = lim * dt * float(m)
        return mag * jnp.cos(ang), mag * jnp.sin(ang)

    pows = [lam_pow(m) for m in range(S5_T + 1)]
    nr, ni = pows[1][0] - 1.0, pows[1][1]
    den = lre * lre + lim * lim
    cfr = (nr * lre + ni * lim) / den
    cfi = (ni * lre - nr * lim) / den
    btr, bti = btr_ref[...], bti_ref[...]
    bbr = cfr * btr - cfi * bti
    bbi = cfr * bti + cfi * btr
    cr, ci = cr_ref[...], ci_ref[...]

    for j in range(S5_T):
        pr, pi = pows[S5_T - 1 - j]
        wz_ref[j * LANES:(j + 1) * LANES, 0:S5_SB] = (pr * bbr - pi * bbi).astype(wz_ref.dtype)
        wz_ref[j * LANES:(j + 1) * LANES, S5_SB:] = (pr * bbi + pi * bbr).astype(wz_ref.dtype)
    for i in range(S5_T):
        pr, pi = pows[i + 1]
        vt_ref[i * LANES:(i + 1) * LANES, 0:S5_SB] = (cr * pr - ci * pi).astype(vt_ref.dtype)
        vt_ref[i * LANES:(i + 1) * LANES, S5_SB:] = (-(cr * pi + ci * pr)).astype(vt_ref.dtype)
    bb = jnp.concatenate([bbr, bbi], axis=1)
    kts = []
    for tau in range(S5_T):
        pr, pi = pows[tau]
        ct = jnp.concatenate([cr * pr - ci * pi, -(cr * pi + ci * pr)], axis=1)
        kts.append(lax.dot_general(bb, ct, NT_DIMS, precision=lax.Precision.HIGHEST,
                                   preferred_element_type=F32))
    zero = jnp.zeros((LANES, LANES), F32)
    for r in range(S5_T):
        for e in range(2):
            tau = S5_T - 2 - r + e
            kt = kts[tau] if tau >= 0 else zero
            g_ref[r * LANES:(r + 1) * LANES, e * LANES:(e + 1) * LANES] = kt.astype(g_ref.dtype)
    for r in range(2 * SUBLANES):
        pr, pi = lam_pow(S5_T * r) if r <= SUBLANES else (jnp.zeros_like(lre), jnp.zeros_like(lre))
        lp_ref[r:r + 1, 0:S5_SB] = pr
        lp_ref[r:r + 1, S5_SB:] = pi


def _s5_prep(lam_re, lam_im, log_dt, b_re, b_im, c_re, c_im):
    ng = lam_re.shape[0]
    nblk = ng // S5_GPB
    eye = jnp.eye(S5_GPB, dtype=F32)

    def rows(t):
        return t.astype(F32).reshape(nblk, 1, S5_SB)

    def embed(t):
        t = t.astype(F32).reshape(nblk, S5_GPB, S5_GROUP, S5_STATE)
        t = t[:, :, :, None, :] * eye[None, :, None, :, None]
        return t.reshape(nblk, LANES, S5_SB)

    dt_rows = jnp.broadcast_to(log_dt.astype(F32)[:, None], (ng, S5_STATE))
    ins = [rows(lam_re), rows(lam_im), rows(dt_rows),
           embed(jnp.swapaxes(b_re, 1, 2)), embed(jnp.swapaxes(b_im, 1, 2)),
           embed(c_re), embed(c_im)]
    row_spec = pl.BlockSpec((None, 1, S5_SB), lambda k: (k, 0, 0))
    mat_spec = pl.BlockSpec((None, LANES, S5_SB), lambda k: (k, 0, 0))
    kt = S5_T * LANES
    out_shape = [jax.ShapeDtypeStruct((nblk, kt, 2 * LANES), BF16),
                 jax.ShapeDtypeStruct((nblk, kt, 2 * S5_SB), BF16),
                 jax.ShapeDtypeStruct((nblk, kt, 2 * S5_SB), BF16),
                 jax.ShapeDtypeStruct((nblk, 2 * SUBLANES, 2 * S5_SB), F32)]
    out_specs = [pl.BlockSpec((None,) + s.shape[1:], lambda k: (k, 0, 0)) for s in out_shape]
    return pl.pallas_call(
        _s5_prep_kernel, grid=(nblk,),
        in_specs=[row_spec] * 3 + [mat_spec] * 4, out_specs=out_specs, out_shape=out_shape,
        compiler_params=_cparams(("parallel",), 48), name="s5_prep",
    )(*ins)


def _s5_kernel(u_ref, g_ref, wz_ref, vt_ref, lp_ref, d_ref, o_ref, u16_ref, z_ref, s_ref):
    nc = u16_ref.shape[0]
    for j in range(S5_T):
        u16_ref[:, j * LANES:(j + 1) * LANES] = u_ref[pl.ds(j, nc, stride=S5_T), :].astype(BF16)
    z_ref[...] = jnp.dot(u16_ref[...], wz_ref[...], preferred_element_type=F32)

    row = lax.broadcasted_iota(jnp.int32, (SUBLANES, S5_SB), 0)

    def shift(x, sh):
        return jnp.where(row >= sh, pltpu.roll(x, sh, axis=0), 0.0)

    def lp(r0, r1):
        return lp_ref[r0:r1, 0:S5_SB], lp_ref[r0:r1, S5_SB:]

    def scan_rows(bi, carry):
        cr, ci = carry
        r0 = pl.multiple_of(bi * SUBLANES, SUBLANES)
        xr = z_ref[pl.ds(r0, SUBLANES), 0:S5_SB]
        xi = z_ref[pl.ds(r0, SUBLANES), S5_SB:]
        for sh in (1, 2, 4):
            pr, pi = lp(sh, sh + 1)
            sr, si = shift(xr, sh), shift(xi, sh)
            xr, xi = xr + pr * sr - pi * si, xi + pr * si + pi * sr
        pr, pi = lp(0, SUBLANES)
        s_ref[pl.ds(r0, SUBLANES), 0:S5_SB] = pr * cr - pi * ci + shift(xr, 1)
        s_ref[pl.ds(r0, SUBLANES), S5_SB:] = pr * ci + pi * cr + shift(xi, 1)
        pr, pi = lp(SUBLANES, SUBLANES + 1)
        last = SUBLANES - 1
        return (pr * cr - pi * ci + xr[last:last + 1], pr * ci + pi * cr + xi[last:last + 1])

    zero = jnp.zeros((1, S5_SB), F32)
    lax.fori_loop(0, nc // SUBLANES, scan_rows, (zero, zero))

    sb = s_ref[...].astype(BF16)
    d = d_ref[...]
    for ip in range(S5_T // 2):
        kk = (2 * ip + 2) * LANES
        acc = jnp.dot(u16_ref[:, 0:kk], g_ref[(S5_T - 2 - 2 * ip) * LANES:, :],
                      preferred_element_type=F32)
        acc = acc + lax.dot_general(sb, vt_ref[2 * ip * LANES:(2 * ip + 2) * LANES, :], NT_DIMS,
                                    preferred_element_type=F32)
        for e in range(2):
            i = 2 * ip + e
            ui = u_ref[pl.ds(i, nc, stride=S5_T), :]
            o_ref[pl.ds(i, nc, stride=S5_T), :] = acc[:, e * LANES:(e + 1) * LANES] + d * ui


def _s5_apply(a_in, gm, wz, vt, lp, d):
    bsz, seq, width = a_in.shape
    nblk = width // LANES
    nc = seq // S5_T
    kt = S5_T * LANES
    act_spec = pl.BlockSpec((None, seq, LANES), lambda k, b: (b, 0, k))

    def wspec(shape):
        return pl.BlockSpec((None,) + shape, lambda k, b: (k, 0, 0))

    return pl.pallas_call(
        _s5_kernel, grid=(nblk, bsz),
        in_specs=[act_spec, wspec((kt, 2 * LANES)), wspec((kt, 2 * S5_SB)), wspec((kt, 2 * S5_SB)),
                  wspec((2 * SUBLANES, 2 * S5_SB)), pl.BlockSpec((1, LANES), lambda k, b: (0, k))],
        out_specs=act_spec,
        out_shape=jax.ShapeDtypeStruct((bsz, seq, width), F32),
        scratch_shapes=[pltpu.VMEM((nc, kt), BF16), pltpu.VMEM((nc, 2 * S5_SB), F32),
                        pltpu.VMEM((nc, 2 * S5_SB), F32)],
        compiler_params=_cparams(("parallel", "parallel"), 56), name="s5_apply",
    )(a_in, gm, wz, vt, lp, d.reshape(1, width))


def _even_tail_kernel(x_ref, ya_ref, az_ref, bv_ref, bg_ref, bz_ref,
                      wglu_ref, bglu_ref, cw_ref, cb_ref, lng_ref, lnb_ref,
                      wpw_ref, bpw_ref, wout_ref, gpost_ref, o_ref, h_ref, hs_ref, c_ref, *, rows):
    tm, w = c_ref.shape
    nsh = hs_ref.shape[1]

    @pl.when(pl.program_id(1) == 0)
    def _():
        h_ref[0:CONV_HALO, :] = jnp.zeros((CONV_HALO, w), F32)

    h_ref[CONV_HALO:, :] = bv_ref[...].astype(F32) * _sigmoid(bg_ref[...].astype(F32))
    for s in range(1, SUBLANES):
        hs_ref[s - 1] = h_ref[s:s + nsh, :]

    off = CONV_HALO - (CONV_K - 1)

    def conv_rows(ri, _):
        r0 = pl.multiple_of(ri * rows, rows)
        for c0 in range(0, w, LANES):
            acc = jnp.broadcast_to(cb_ref[:, c0:c0 + LANES], (rows, LANES))
            for s in range(SUBLANES):
                taps = [o for o in range(off, off + CONV_K) if o % SUBLANES == s]
                src = h_ref if s == 0 else hs_ref.at[s - 1]
                lo, hi = taps[0] - s, taps[-1] - s + rows
                win = src[pl.ds(pl.multiple_of(r0 + lo, SUBLANES), hi - lo), c0:c0 + LANES]
                for o in taps:
                    k = o - off
                    acc = acc + cw_ref[k:k + 1, c0:c0 + LANES] * win[o - s - lo:o - s - lo + rows]
            c_ref[pl.ds(r0, rows), c0:c0 + LANES] = acc
        return 0

    lax.fori_loop(0, tm // rows, conv_rows, 0)
    h_ref[0:CONV_HALO, :] = h_ref[tm:tm + CONV_HALO, :]

    hc = c_ref[...]
    mu = jnp.mean(hc, axis=-1, keepdims=True)
    xc = hc - mu
    hn = xc * lax.rsqrt(jnp.mean(xc * xc, axis=-1, keepdims=True) + EPS) * lng_ref[...] + lnb_ref[...]
    yb = jnp.dot(_silu(hn).astype(BF16), wpw_ref[...], preferred_element_type=F32) + bpw_ref[...]
    yb = yb * _silu(bz_ref[...].astype(F32))

    ya = jax.nn.gelu(ya_ref[...])
    gate = jnp.dot(ya.astype(BF16), wglu_ref[...], preferred_element_type=F32) + bglu_ref[...]
    ya = ya * _sigmoid(gate) * _silu(az_ref[...].astype(F32))

    y = jnp.dot(ya.astype(BF16), wout_ref[0:w, :], preferred_element_type=F32)
    y = y + jnp.dot(yb.astype(BF16), wout_ref[w:, :], preferred_element_type=F32)
    o_ref[...] = x_ref[...] + y * _rms_scale(y) * gpost_ref[...]


def _even_tail(x, ya, p, wglu, bglu, cw, cb, lng, lnb, wpw, bpw, wout, gpost, tm=256, rows=32):
    bsz, seq, d = x.shape
    w = ya.shape[-1]

    def act(col):
        return pl.BlockSpec((None, tm, w), lambda b, t: (b, t, col))

    vec = lambda v: v.reshape(1, -1).astype(F32)
    cw_pad = jnp.zeros((CONV_HALO, w), F32).at[0:CONV_K].set(cw.astype(F32))
    consts = [wglu, vec(bglu), cw_pad, vec(cb), vec(lng), vec(lnb), wpw, vec(bpw), wout, vec(gpost)]
    return pl.pallas_call(
        functools.partial(_even_tail_kernel, rows=rows), grid=(bsz, seq // tm),
        in_specs=[act(0), act(0), act(0), act(1), act(2), act(3)] + [_const_spec(c.shape) for c in consts],
        out_specs=act(0),
        out_shape=jax.ShapeDtypeStruct((bsz, seq, d), F32),
        scratch_shapes=[pltpu.VMEM((CONV_HALO + tm, w), F32),
                        pltpu.VMEM((SUBLANES - 1, CONV_HALO + tm - SUBLANES, w), F32),
                        pltpu.VMEM((tm, w), F32)],
        compiler_params=_cparams(("parallel", "arbitrary"), 56), name="even_tail",
    )(x, ya, p, p, p, p, *consts)


def _gla_head(q_b, k_b, v, r_b, lr, wg, bg, ng, st_ref):
    tb, hk = q_b.shape
    c = GLA_CHUNK
    half = 2 * c
    assert tb == 2 * half
    tpc = c // SUBLANES

    gp = jnp.dot(lr, wg, preferred_element_type=F32) + bg
    g = (jnp.minimum(gp, 0.0) - jnp.log1p(jnp.exp(-jnp.abs(gp)))) * (1.0 / GLA_TAU)

    row8 = lax.broadcasted_iota(jnp.int32, (SUBLANES, hk), 0)
    tiles = []
    for i in range(tb // SUBLANES):
        x = g[i * SUBLANES:(i + 1) * SUBLANES]
        for sh in (1, 2, 4):
            x = x + jnp.where(row8 >= sh, pltpu.roll(x, sh, axis=0), 0.0)
        if i % tpc:
            x = x + tiles[-1][SUBLANES - 1:SUBLANES, :]
        tiles.append(x)
    bc = jnp.concatenate(tiles, axis=0)
    bl = [tiles[(a + 1) * tpc - 1][SUBLANES - 1:SUBLANES, :] for a in range(tb // c)]
    bl_rows = jnp.concatenate([jnp.broadcast_to(b, (c, hk)) for b in bl], axis=0)

    q = q_b.astype(F32) * (GLA_HK ** -0.5)
    k = k_b.astype(F32)
    qd = q * jnp.exp(bc)
    kd = k * jnp.exp(-bc)
    ku = k * jnp.exp(bl_rows - bc)
    qd_b, kd_b, ku_b = qd.astype(BF16), kd.astype(BF16), ku.astype(BF16)

    def rows(t, a, n=1):
        return t[a * c:(a + n) * c]

    e1, e2, e3 = jnp.exp(bl[1]), jnp.exp(bl[2]), jnp.exp(bl[3])
    e01 = jnp.exp(bl[0] + bl[1])
    e012 = jnp.exp(bl[0] + bl[1] + bl[2])
    e0123 = jnp.exp(bl[0] + bl[1] + bl[2] + bl[3])
    e23 = jnp.exp(bl[2] + bl[3])
    e123 = jnp.exp(bl[1] + bl[2] + bl[3])
    e0 = jnp.exp(bl[0])
    qx = jnp.concatenate([rows(qd_b, 0), (rows(qd, 1) * e0).astype(BF16), (rows(qd, 2) * e01).astype(BF16),
                          (rows(qd, 3) * e012).astype(BF16)], axis=0)
    kx = jnp.concatenate([(rows(ku, 0) * e123).astype(BF16), (rows(ku, 1) * e23).astype(BF16),
                          (rows(ku, 2) * e3).astype(BF16), rows(ku_b, 3)], axis=0)
    q_hi = jnp.concatenate([rows(qd_b, 2), (rows(qd, 3) * e2).astype(BF16)], axis=0)
    k_lo = jnp.concatenate([(rows(ku, 0) * e1).astype(BF16), rows(ku_b, 1)], axis=0)
    cross = lax.dot_general(q_hi, k_lo, NT_DIMS, preferred_element_type=F32).astype(BF16)

    ri = lax.broadcasted_iota(jnp.int32, (half, half), 0)
    ci = lax.broadcasted_iota(jnp.int32, (half, half), 1)
    causal = ri >= ci

    def half_attn(h):
        a0, a1 = 2 * h, 2 * h + 1
        top = lax.dot_general(rows(qd_b, a0), rows(kd_b, a0, 2), NT_DIMS, preferred_element_type=F32)
        k_bot = jnp.concatenate([rows(ku_b, a0), rows(kd_b, a1)], axis=0)
        bot = lax.dot_general(rows(qd_b, a1), k_bot, NT_DIMS, preferred_element_type=F32)
        return jnp.where(causal, jnp.concatenate([top, bot], axis=0), 0.0).astype(BF16)

    st = st_ref[...]
    o_top = jnp.dot(half_attn(0), v[0:half], preferred_element_type=F32)
    o_bot = jnp.dot(jnp.concatenate([cross, half_attn(1)], axis=1), v, preferred_element_type=F32)
    o = jnp.concatenate([o_top, o_bot], axis=0)
    o = o + lax.dot_general(qx, st.astype(BF16), NT_DIMS, preferred_element_type=F32)
    st_ref[...] = st * e0123 + lax.dot_general(v, kx, TN_DIMS, preferred_element_type=F32)
    on = o * _rms_scale(o) * ng
    return on * _silu(r_b.astype(F32))


def _gla_kernel(q_ref, k_ref, v_ref, r_ref, lr_ref, wg_ref, bg_ref, ng_ref, o_ref, st_ref):
    @pl.when(pl.program_id(2) == 0)
    def _():
        st_ref[...] = jnp.zeros_like(st_ref)

    hk, hv = GLA_HK, GLA_HV
    lr, ng = lr_ref[...], ng_ref[...]
    for j in range(st_ref.shape[0]):
        ks, vs = slice(j * hk, (j + 1) * hk), slice(j * hv, (j + 1) * hv)
        out = _gla_head(q_ref[:, ks], k_ref[:, ks], v_ref[:, vs], r_ref[:, vs], lr,
                        wg_ref[:, ks], bg_ref[:, ks], ng, st_ref.at[j])
        o_ref[:, vs] = out.astype(o_ref.dtype)


def _gla(p, wg, bg, ng, tb=256, hps=2):
    bsz, seq, _ = p.shape
    nh, hk, hv = GLA_HEADS, GLA_HK * hps, GLA_HV * hps
    dk, dv = GLA_HEADS * GLA_HK, GLA_HEADS * GLA_HV

    def spec(width, base):
        return pl.BlockSpec((None, tb, width), lambda b, h, t: (b, t, base + h))

    lr_col = (2 * dk + 2 * dv) // LANES
    in_specs = [spec(hk, 0), spec(hk, dk // hk), spec(hv, 2 * dk // hv), spec(hv, (2 * dk + dv) // hv),
                pl.BlockSpec((None, tb, LANES), lambda b, h, t: (b, t, lr_col)),
                pl.BlockSpec((LANES, hk), lambda b, h, t: (0, h)),
                pl.BlockSpec((1, hk), lambda b, h, t: (0, h)),
                pl.BlockSpec((1, GLA_HV), lambda b, h, t: (0, 0))]
    wg_pad = jnp.zeros((LANES, dk), BF16).at[0:GLA_LOWRANK].set(wg.astype(BF16))
    return pl.pallas_call(
        _gla_kernel, grid=(bsz, nh // hps, seq // tb), in_specs=in_specs,
        out_specs=pl.BlockSpec((None, tb, hv), lambda b, h, t: (b, t, h)),
        out_shape=jax.ShapeDtypeStruct((bsz, seq, dv), BF16),
        scratch_shapes=[pltpu.VMEM((hps, GLA_HV, GLA_HK), F32)],
        compiler_params=_cparams(("parallel", "parallel", "arbitrary"), 48), name="gla",
    )(p, p, p, p, p, wg_pad, bg.reshape(1, dk).astype(F32), ng.reshape(1, GLA_HV).astype(F32))


def _out_proj_kernel(x_ref, y_ref, w_ref, g_ref, o_ref):
    y = jnp.dot(y_ref[...], w_ref[...], preferred_element_type=F32)
    o_ref[...] = x_ref[...] + y * _rms_scale(y) * g_ref[...]


def _out_proj(x2, y2, w, g, tm=512):
    m, d = x2.shape
    kdim = y2.shape[1]
    return pl.pallas_call(
        _out_proj_kernel, grid=(m // tm,),
        in_specs=[pl.BlockSpec((tm, d), lambda i: (i, 0)), pl.BlockSpec((tm, kdim), lambda i: (i, 0)),
                  _const_spec(w.shape), _const_spec((1, d))],
        out_specs=pl.BlockSpec((tm, d), lambda i: (i, 0)),
        out_shape=jax.ShapeDtypeStruct((m, d), F32),
        compiler_params=_cparams(("parallel",), 48), name="out_proj",
    )(x2, y2, w, g.reshape(1, d).astype(F32))


def _even_layer(x, g_pre, g_post, w_in, s5p, s5_d, w_glu, b_glu, cw, cb, lng, lnb, w_pw, b_pw, w_out):
    bsz, seq, d = x.shape
    wa = s5_d.shape[0]
    w_in = w_in.astype(BF16)
    a_in, rest = _norm_proj(x.reshape(bsz * seq, d), g_pre.astype(F32),
                            [w_in[:, :wa], w_in[:, wa:]], [F32, BF16])
    gm, wz, vt, lp = _s5_prep(*s5p)
    ya = _s5_apply(a_in.reshape(bsz, seq, wa), gm, wz, vt, lp, s5_d.astype(F32))
    return _even_tail(x, ya, rest.reshape(bsz, seq, -1), w_glu.astype(BF16), b_glu, cw, cb, lng, lnb,
                      w_pw.astype(BF16), b_pw, w_out.astype(BF16), g_post)


def _odd_layer(x, g_pre, g_post, w_in, wg, bg, ng, w_out):
    bsz, seq, d = x.shape
    n_in = w_in.shape[1]
    n_pad = -(-n_in // LANES) * LANES
    w_pad = jnp.zeros((d, n_pad), BF16).at[:, :n_in].set(w_in.astype(BF16))
    (p,) = _norm_proj(x.reshape(bsz * seq, d), g_pre.astype(F32), [w_pad], [BF16])
    y = _gla(p.reshape(bsz, seq, n_pad), wg, bg, ng)
    out = _out_proj(x.reshape(bsz * seq, d), y.reshape(bsz * seq, -1), w_out.astype(BF16), g_post)
    return out.reshape(bsz, seq, d)


def kernel(x, norm_pre, norm_post, ev_w_in, s5_lambda_re, s5_lambda_im, s5_log_dt, s5_b_re, s5_b_im, s5_c_re, s5_c_im, s5_d, s5_w_glu, s5_b_glu, conv_w, conv_b, conv_ln_g, conv_ln_b, conv_w_pw, conv_b_pw, ev_w_out, od_w_in, gla_w_gate_up, gla_b_gate, gla_norm_g, od_w_out):
    depth = norm_pre.shape[0]
    for i in range(depth):
        j = i // 2
        if i % 2 == 0:
            s5p = (s5_lambda_re[j], s5_lambda_im[j], s5_log_dt[j], s5_b_re[j], s5_b_im[j],
                   s5_c_re[j], s5_c_im[j])
            x = _even_layer(x, norm_pre[i], norm_post[i], ev_w_in[j], s5p, s5_d[j], s5_w_glu[j],
                            s5_b_glu[j], conv_w[j], conv_b[j], conv_ln_g[j], conv_ln_b[j],
                            conv_w_pw[j], conv_b_pw[j], ev_w_out[j])
        else:
            x = _odd_layer(x, norm_pre[i], norm_post[i], od_w_in[j], gla_w_gate_up[j], gla_b_gate[j],
                           gla_norm_g[j], od_w_out[j])
    return x
```

```python
import functools
import math

import jax
import jax.numpy as jnp
from jax import lax
from jax.experimental import pallas as pl
from jax.experimental.pallas import tpu as pltpu

F32 = jnp.float32
BF16 = jnp.bfloat16

EPS = 1e-6
LANES = 128
SUBLANES = 8

S5_GROUP = 16
S5_STATE = 64
S5_T = 16
S5_GPB = LANES // S5_GROUP
S5_SB = S5_GPB * S5_STATE
CONV_K = 31
CONV_HALO = 32
GLA_HEADS = 4
GLA_HK = 256
GLA_HV = 512
GLA_LOWRANK = 16
GLA_TAU = 16.0
GLA_CHUNK = 64

NT_DIMS = (((1,), (1,)), ((), ()))
TN_DIMS = (((0,), (0,)), ((), ()))


def _cparams(sem, vmem_mb):
    return pltpu.CompilerParams(dimension_semantics=sem, vmem_limit_bytes=vmem_mb * 1024 * 1024)


def _const_spec(shape):
    nd = len(shape)
    return pl.BlockSpec(shape, lambda *_: (0,) * nd, pipeline_mode=pl.Buffered(1))


def _rms_scale(x):
    return lax.rsqrt(jnp.mean(x * x, axis=-1, keepdims=True) + EPS)


def _sigmoid(x):
    return 0.5 * jnp.tanh(0.5 * x) + 0.5


def _silu(x):
    return x * _sigmoid(x)


def _layer_spec(shape, layer):
    nd = len(shape)
    return pl.BlockSpec((None,) + tuple(shape), lambda *_: (layer,) + (0,) * nd, pipeline_mode=pl.Buffered(1))


def _norm_proj_kernel(x_ref, g_ref, *refs, n_w, splits, n_chunk):
    w_refs, o_refs = refs[:n_w], refs[n_w:]
    x = x_ref[...]
    u = (x * _rms_scale(x) * g_ref[...]).astype(BF16)
    outs = iter(o_refs)
    for w_ref, widths in zip(w_refs, splits):
        base = 0
        for width in widths:
            o_ref = next(outs)
            for n0 in range(0, width, n_chunk):
                n1 = min(n0 + n_chunk, width)
                o_ref[:, n0:n1] = jnp.dot(u, w_ref[:, base + n0:base + n1],
                                          preferred_element_type=F32).astype(o_ref.dtype)
            base += width


def _norm_proj(x2, g_stack, gi, w_stacks, wi, splits, out_dtypes, tm=512, n_chunk=512):
    m, d = x2.shape
    widths = [wd for ws in splits for wd in ws]
    in_specs = [pl.BlockSpec((tm, d), lambda i: (i, 0)), _layer_spec((1, d), gi)]
    in_specs += [_layer_spec(w.shape[1:], wi) for w in w_stacks]
    out_specs = [pl.BlockSpec((tm, wd), lambda i: (i, 0)) for wd in widths]
    out_shape = [jax.ShapeDtypeStruct((m, wd), dt) for wd, dt in zip(widths, out_dtypes)]
    return pl.pallas_call(
        functools.partial(_norm_proj_kernel, n_w=len(w_stacks), splits=splits, n_chunk=n_chunk),
        grid=(m // tm,), in_specs=in_specs, out_specs=out_specs, out_shape=out_shape,
        compiler_params=_cparams(("parallel",), 56), name="norm_proj",
    )(x2, g_stack.reshape(g_stack.shape[0], 1, d), *w_stacks)


def _s5_prep_kernel(lre_ref, lim_ref, dt_ref, btr_ref, bti_ref, cr_ref, ci_ref,
                    g_ref, wz_ref, vt_ref, lp_ref):
    lre, lim, dt = lre_ref[...], lim_ref[...], jnp.exp(dt_ref[...])

    def lam_pow(m):
        mag = jnp.exp(lre * dt * float(m))
        ang = lim * dt * float(m)
        return mag * jnp.cos(ang), mag * jnp.sin(ang)

    pows = [lam_pow(m) for m in range(S5_T + 1)]
    nr, ni = pows[1][0] - 1.0, pows[1][1]
    den = lre * lre + lim * lim
    cfr = (nr * lre + ni * lim) / den
    cfi = (ni * lre - nr * lim) / den
    btr, bti = btr_ref[...], bti_ref[...]
    bbr = cfr * btr - cfi * bti
    bbi = cfr * bti + cfi * btr
    cr, ci = cr_ref[...], ci_ref[...]

    for j in range(S5_T):
        pr, pi = pows[S5_T - 1 - j]
        wz_ref[j * LANES:(j + 1) * LANES, 0:S5_SB] = (pr * bbr - pi * bbi).astype(wz_ref.dtype)
        wz_ref[j * LANES:(j + 1) * LANES, S5_SB:] = (pr * bbi + pi * bbr).astype(wz_ref.dtype)
    for i in range(S5_T):
        pr, pi = pows[i + 1]
        vt_ref[i * LANES:(i + 1) * LANES, 0:S5_SB] = (cr * pr - ci * pi).astype(vt_ref.dtype)
        vt_ref[i * LANES:(i + 1) * LANES, S5_SB:] = (-(cr * pi + ci * pr)).astype(vt_ref.dtype)
    def split(t):
        hi = t.astype(BF16)
        return hi, (t - hi.astype(F32)).astype(BF16)

    def dot_nt(a, b):
        return lax.dot_general(a, b, NT_DIMS, preferred_element_type=F32)

    bb_hi, bb_lo = split(jnp.concatenate([bbr, bbi], axis=1))
    kts = []
    for tau in range(S5_T):
        pr, pi = pows[tau]
        ct_hi, ct_lo = split(jnp.concatenate([cr * pr - ci * pi, -(cr * pi + ci * pr)], axis=1))
        kts.append(dot_nt(bb_hi, ct_hi) + dot_nt(bb_hi, ct_lo) + dot_nt(bb_lo, ct_hi))
    zero = jnp.zeros((LANES, LANES), F32)
    for r in range(S5_T):
        for e in range(2):
            tau = S5_T - 2 - r + e
            kt = kts[tau] if tau >= 0 else zero
            g_ref[r * LANES:(r + 1) * LANES, e * LANES:(e + 1) * LANES] = kt.astype(g_ref.dtype)
    for r in range(2 * SUBLANES):
        pr, pi = lam_pow(S5_T * r) if r <= SUBLANES else (jnp.zeros_like(lre), jnp.zeros_like(lre))
        lp_ref[r:r + 1, 0:S5_SB] = pr
        lp_ref[r:r + 1, S5_SB:] = pi


def _s5_prep(lam_re, lam_im, log_dt, b_re, b_im, c_re, c_im):
    ng = lam_re.shape[0]
    nblk = ng // S5_GPB
    eye = jnp.eye(S5_GPB, dtype=F32)

    def rows(t):
        return t.astype(F32).reshape(nblk, 1, S5_SB)

    def embed(t):
        t = t.astype(F32).reshape(nblk, S5_GPB, S5_GROUP, S5_STATE)
        t = t[:, :, :, None, :] * eye[None, :, None, :, None]
        return t.reshape(nblk, LANES, S5_SB)

    dt_rows = jnp.broadcast_to(log_dt.astype(F32)[:, None], (ng, S5_STATE))
    ins = [rows(lam_re), rows(lam_im), rows(dt_rows),
           embed(jnp.swapaxes(b_re, 1, 2)), embed(jnp.swapaxes(b_im, 1, 2)),
           embed(c_re), embed(c_im)]
    row_spec = pl.BlockSpec((None, 1, S5_SB), lambda k: (k, 0, 0))
    mat_spec = pl.BlockSpec((None, LANES, S5_SB), lambda k: (k, 0, 0))
    kt = S5_T * LANES
    out_shape = [jax.ShapeDtypeStruct((nblk, kt, 2 * LANES), BF16),
                 jax.ShapeDtypeStruct((nblk, kt, 2 * S5_SB), BF16),
                 jax.ShapeDtypeStruct((nblk, kt, 2 * S5_SB), BF16),
                 jax.ShapeDtypeStruct((nblk, 2 * SUBLANES, 2 * S5_SB), F32)]
    out_specs = [pl.BlockSpec((None,) + s.shape[1:], lambda k: (k, 0, 0)) for s in out_shape]
    return pl.pallas_call(
        _s5_prep_kernel, grid=(nblk,),
        in_specs=[row_spec] * 3 + [mat_spec] * 4, out_specs=out_specs, out_shape=out_shape,
        compiler_params=_cparams(("parallel",), 48), name="s5_prep",
    )(*ins)


def _s5_kernel(u_ref, g_ref, wz_ref, vt_ref, lp_ref, d_ref, o_ref, u16_ref, z_ref, s_ref, y_ref):
    nc = u16_ref.shape[0]
    kq = S5_T * LANES // 4
    z = None
    for q in range(4):
        for j in range(q * S5_T // 4, (q + 1) * S5_T // 4):
            u16_ref[:, j * LANES:(j + 1) * LANES] = u_ref[pl.ds(j, nc, stride=S5_T), :].astype(BF16)
        zq = jnp.dot(u16_ref[:, q * kq:(q + 1) * kq], wz_ref[q * kq:(q + 1) * kq, :],
                     preferred_element_type=F32)
        z = zq if z is None else z + zq
    z_ref[...] = z

    row = lax.broadcasted_iota(jnp.int32, (SUBLANES, S5_SB), 0)

    def shift(x, sh):
        return jnp.where(row >= sh, pltpu.roll(x, sh, axis=0), 0.0)

    def lp(r0, r1):
        return lp_ref[r0:r1, 0:S5_SB], lp_ref[r0:r1, S5_SB:]

    def scan_rows(r0, carry):
        cr, ci = carry
        xr = z_ref[r0:r0 + SUBLANES, 0:S5_SB]
        xi = z_ref[r0:r0 + SUBLANES, S5_SB:]
        for sh in (1, 2, 4):
            pr, pi = lp(sh, sh + 1)
            sr, si = shift(xr, sh), shift(xi, sh)
            xr, xi = xr + pr * sr - pi * si, xi + pr * si + pi * sr
        pr, pi = lp(0, SUBLANES)
        s_ref[r0:r0 + SUBLANES, 0:S5_SB] = (pr * cr - pi * ci + shift(xr, 1)).astype(s_ref.dtype)
        s_ref[r0:r0 + SUBLANES, S5_SB:] = (pr * ci + pi * cr + shift(xi, 1)).astype(s_ref.dtype)
        pr, pi = lp(SUBLANES, SUBLANES + 1)
        last = SUBLANES - 1
        return (pr * cr - pi * ci + xr[last:last + 1], pr * ci + pi * cr + xi[last:last + 1])

    pairs = S5_T // 2
    for ip in range(pairs):
        kk = (2 * ip + 2) * LANES
        y_ref[:, 2 * ip * LANES:(2 * ip + 2) * LANES] = jnp.dot(
            u16_ref[:, 0:kk], g_ref[(S5_T - 2 - 2 * ip) * LANES:, :], preferred_element_type=F32)

    carry = (jnp.zeros((1, S5_SB), F32), jnp.zeros((1, S5_SB), F32))
    for r0 in range(0, nc, SUBLANES):
        carry = scan_rows(r0, carry)

    sb = s_ref[...].astype(BF16)
    d = d_ref[...]
    for ip in range(pairs):
        acc = y_ref[:, 2 * ip * LANES:(2 * ip + 2) * LANES] + lax.dot_general(
            sb, vt_ref[2 * ip * LANES:(2 * ip + 2) * LANES, :], NT_DIMS, preferred_element_type=F32)
        for e in range(2):
            i = 2 * ip + e
            ui = u_ref[pl.ds(i, nc, stride=S5_T), :]
            o_ref[pl.ds(i, nc, stride=S5_T), :] = acc[:, e * LANES:(e + 1) * LANES] + d * ui


def _s5_apply(a_in, gm, wz, vt, lp, d):
    bsz, seq, width = a_in.shape
    nblk = width // LANES
    nc = seq // S5_T
    kt = S5_T * LANES
    act_spec = pl.BlockSpec((None, seq, LANES), lambda k, b: (b, 0, k))

    def wspec(shape):
        return pl.BlockSpec((None,) + shape, lambda k, b: (k, 0, 0))

    return pl.pallas_call(
        _s5_kernel, grid=(nblk, bsz),
        in_specs=[act_spec, wspec((kt, 2 * LANES)), wspec((kt, 2 * S5_SB)), wspec((kt, 2 * S5_SB)),
                  wspec((2 * SUBLANES, 2 * S5_SB)), pl.BlockSpec((1, LANES), lambda k, b: (0, k))],
        out_specs=act_spec,
        out_shape=jax.ShapeDtypeStruct((bsz, seq, width), F32),
        scratch_shapes=[pltpu.VMEM((nc, kt), BF16), pltpu.VMEM((nc, 2 * S5_SB), F32),
                        pltpu.VMEM((nc, 2 * S5_SB), F32), pltpu.VMEM((nc, kt), F32)],
        compiler_params=_cparams(("parallel", "parallel"), 56), name="s5_apply",
    )(a_in, gm, wz, vt, lp, d.reshape(1, width))


def _even_tail_kernel(x_ref, ya_ref, az_ref, bv_ref, bg_ref, bz_ref,
                      wglu_ref, bglu_ref, cw_ref, cb_ref, lng_ref, lnb_ref,
                      wpw_ref, bpw_ref, wout_ref, gpost_ref, o_ref, h_ref, hs_ref, c_ref, *, rows):
    tm, w = c_ref.shape
    nsh = hs_ref.shape[1]

    @pl.when(pl.program_id(1) == 0)
    def _():
        h_ref[0:CONV_HALO, :] = jnp.zeros((CONV_HALO, w), F32)

    h_ref[CONV_HALO:, :] = bv_ref[...].astype(F32) * _sigmoid(bg_ref[...].astype(F32))
    for s in range(1, SUBLANES):
        hs_ref[s - 1] = h_ref[s:s + nsh, :]

    off = CONV_HALO - (CONV_K - 1)

    def conv_rows(ri, _):
        r0 = pl.multiple_of(ri * rows, rows)
        for c0 in range(0, w, LANES):
            acc = jnp.broadcast_to(cb_ref[:, c0:c0 + LANES], (rows, LANES))
            for s in range(SUBLANES):
                taps = [o for o in range(off, off + CONV_K) if o % SUBLANES == s]
                src = h_ref if s == 0 else hs_ref.at[s - 1]
                lo, hi = taps[0] - s, taps[-1] - s + rows
                win = src[pl.ds(pl.multiple_of(r0 + lo, SUBLANES), hi - lo), c0:c0 + LANES]
                for o in taps:
                    k = o - off
                    acc = acc + cw_ref[k:k + 1, c0:c0 + LANES] * win[o - s - lo:o - s - lo + rows]
            c_ref[pl.ds(r0, rows), c0:c0 + LANES] = acc
        return 0

    lax.fori_loop(0, tm // rows, conv_rows, 0)
    h_ref[0:CONV_HALO, :] = h_ref[tm:tm + CONV_HALO, :]

    hc = c_ref[...]
    mu = jnp.mean(hc, axis=-1, keepdims=True)
    xc = hc - mu
    hn = xc * lax.rsqrt(jnp.mean(xc * xc, axis=-1, keepdims=True) + EPS) * lng_ref[...] + lnb_ref[...]
    yb = jnp.dot(_silu(hn).astype(BF16), wpw_ref[...], preferred_element_type=F32) + bpw_ref[...]
    yb = yb * _silu(bz_ref[...].astype(F32))

    ya = jax.nn.gelu(ya_ref[...])
    gate = jnp.dot(ya.astype(BF16), wglu_ref[...], preferred_element_type=F32) + bglu_ref[...]
    ya = ya * _sigmoid(gate) * _silu(az_ref[...].astype(F32))

    y = jnp.dot(ya.astype(BF16), wout_ref[0:w, :], preferred_element_type=F32)
    y = y + jnp.dot(yb.astype(BF16), wout_ref[w:, :], preferred_element_type=F32)
    o_ref[...] = x_ref[...] + y * _rms_scale(y) * gpost_ref[...]


def _even_tail(x, ya, p, i, wglu_b, bglu, cw, cb, lng, lnb, wpw_b, bpw, wout_b, norm_post, tm=512, rows=32):
    bsz, seq, d = x.shape
    w = ya.shape[-1]
    j = i // 2

    def act(col):
        return pl.BlockSpec((None, tm, w), lambda b, t: (b, t, col))

    vec = lambda v: v.reshape(1, -1).astype(F32)
    cw_pad = jnp.pad(cw.astype(F32), ((0, CONV_HALO - CONV_K), (0, 0)))
    consts = [wglu_b, vec(bglu), cw_pad, vec(cb), vec(lng), vec(lnb), wpw_b, vec(bpw), wout_b,
              norm_post.reshape(norm_post.shape[0], 1, d)]
    stacked = {0: j, 6: j, 8: j, 9: i}
    const_specs = [_layer_spec(c.shape[1:], stacked[n]) if n in stacked else _const_spec(c.shape)
                   for n, c in enumerate(consts)]
    return pl.pallas_call(
        functools.partial(_even_tail_kernel, rows=rows), grid=(bsz, seq // tm),
        in_specs=[act(0), act(0), act(0), act(1), act(2), act(3)] + const_specs,
        out_specs=act(0),
        out_shape=jax.ShapeDtypeStruct((bsz, seq, d), F32),
        scratch_shapes=[pltpu.VMEM((CONV_HALO + tm, w), F32),
                        pltpu.VMEM((SUBLANES - 1, CONV_HALO + tm - SUBLANES, w), F32),
                        pltpu.VMEM((tm, w), F32)],
        compiler_params=_cparams(("parallel", "arbitrary"), 56), name="even_tail",
    )(x, ya, p, p, p, p, *consts)


_GLA_C = GLA_CHUNK
_GLA_TB = 4 * _GLA_C
_R_QD, _R_KD, _R_KU, _R_QX, _R_KX = (i * _GLA_TB for i in range(5))
_R_QHI, _R_KLO, _R_KB0, _R_KB1 = (5 * _GLA_TB + i * 2 * _GLA_C for i in range(4))
_GLA_OPS_ROWS = 7 * _GLA_TB


def _gla_prep(q_b, k_b, lr, wg, bg, ops_ref, dec_ref):
    tb, hk = q_b.shape
    c = _GLA_C
    assert tb == _GLA_TB
    tpc = c // SUBLANES

    gp = jnp.dot(lr, wg, preferred_element_type=F32) + bg
    g = (jnp.minimum(gp, 0.0) - jnp.log(1.0 + jnp.exp(-jnp.abs(gp)))) * (1.0 / GLA_TAU)

    row8 = lax.broadcasted_iota(jnp.int32, (SUBLANES, hk), 0)
    tiles = []
    for i in range(tb // SUBLANES):
        x = g[i * SUBLANES:(i + 1) * SUBLANES]
        for sh in (1, 2, 4):
            x = x + jnp.where(row8 >= sh, pltpu.roll(x, sh, axis=0), 0.0)
        if i % tpc:
            x = x + tiles[-1][SUBLANES - 1:SUBLANES, :]
        tiles.append(x)
    bc = jnp.concatenate(tiles, axis=0)
    bl = [tiles[(a + 1) * tpc - 1][SUBLANES - 1:SUBLANES, :] for a in range(tb // c)]
    bl_rows = jnp.concatenate([jnp.broadcast_to(b, (c, hk)) for b in bl], axis=0)

    q = q_b.astype(F32) * (GLA_HK ** -0.5)
    k = k_b.astype(F32)
    qd = q * jnp.exp(bc)
    kd = k * jnp.exp(-bc)
    ku = k * jnp.exp(bl_rows - bc)
    qd_b, kd_b, ku_b = qd.astype(BF16), kd.astype(BF16), ku.astype(BF16)
    ops_ref[_R_QD:_R_QD + tb] = qd_b
    ops_ref[_R_KD:_R_KD + tb] = kd_b
    ops_ref[_R_KU:_R_KU + tb] = ku_b

    def rows(t, a):
        return t[a * c:(a + 1) * c]

    def put(r0, a, val):
        ops_ref[r0 + a * c:r0 + (a + 1) * c] = val

    e0, e1, e2, e3 = (jnp.exp(b) for b in bl)
    e01 = jnp.exp(bl[0] + bl[1])
    e012 = jnp.exp(bl[0] + bl[1] + bl[2])
    e23 = jnp.exp(bl[2] + bl[3])
    e123 = jnp.exp(bl[1] + bl[2] + bl[3])
    dec_ref[...] = jnp.broadcast_to(jnp.exp(bl[0] + bl[1] + bl[2] + bl[3]), dec_ref.shape)
    put(_R_QX, 0, rows(qd_b, 0))
    put(_R_QX, 1, (rows(qd, 1) * e0).astype(BF16))
    put(_R_QX, 2, (rows(qd, 2) * e01).astype(BF16))
    put(_R_QX, 3, (rows(qd, 3) * e012).astype(BF16))
    put(_R_KX, 0, (rows(ku, 0) * e123).astype(BF16))
    put(_R_KX, 1, (rows(ku, 1) * e23).astype(BF16))
    put(_R_KX, 2, (rows(ku, 2) * e3).astype(BF16))
    put(_R_KX, 3, rows(ku_b, 3))
    put(_R_QHI, 0, rows(qd_b, 2))
    put(_R_QHI, 1, (rows(qd, 3) * e2).astype(BF16))
    put(_R_KLO, 0, (rows(ku, 0) * e1).astype(BF16))
    put(_R_KLO, 1, rows(ku_b, 1))
    put(_R_KB0, 0, rows(ku_b, 0))
    put(_R_KB0, 1, rows(kd_b, 1))
    put(_R_KB1, 0, rows(ku_b, 2))
    put(_R_KB1, 1, rows(kd_b, 3))


def _gla_mix(ops_ref, dec_ref, v, r_b, ng, st_ref):
    c = _GLA_C
    half = 2 * c

    def op(r0, n, a=0):
        return ops_ref[r0 + a * c:r0 + (a + n) * c]

    cross = lax.dot_general(op(_R_QHI, 2), op(_R_KLO, 2), NT_DIMS, preferred_element_type=F32).astype(BF16)
    ri = lax.broadcasted_iota(jnp.int32, (half, half), 0)
    ci = lax.broadcasted_iota(jnp.int32, (half, half), 1)
    causal = ri >= ci

    def half_attn(h):
        top = lax.dot_general(op(_R_QD, 1, 2 * h), op(_R_KD, 2, 2 * h), NT_DIMS, preferred_element_type=F32)
        bot = lax.dot_general(op(_R_QD, 1, 2 * h + 1), op(_R_KB1 if h else _R_KB0, 2), NT_DIMS,
                              preferred_element_type=F32)
        return jnp.where(causal, jnp.concatenate([top, bot], axis=0), 0.0).astype(BF16)

    st = st_ref[...]
    o_top = jnp.dot(half_attn(0), v[0:half], preferred_element_type=F32)
    o_bot = jnp.dot(jnp.concatenate([cross, half_attn(1)], axis=1), v, preferred_element_type=F32)
    o = jnp.concatenate([o_top, o_bot], axis=0)
    o = o + lax.dot_general(op(_R_QX, 4), st.astype(BF16), NT_DIMS, preferred_element_type=F32)
    st_ref[...] = st * dec_ref[0:1, :] + lax.dot_general(v, op(_R_KX, 4), TN_DIMS, preferred_element_type=F32)
    on = o * _rms_scale(o) * ng
    return on * _silu(r_b.astype(F32))


def _gla_kernel(q_ref, k_ref, v_ref, r_ref, lr_ref, wg_ref, bg_ref, ng_ref, o_ref, st_ref, ops_ref, dec_ref):
    @pl.when(pl.program_id(2) == 0)
    def _():
        st_ref[...] = jnp.zeros_like(st_ref)

    hk, hv = GLA_HK, GLA_HV
    nheads = st_ref.shape[0]
    lr, ng = lr_ref[...], ng_ref[...]

    def prep(j):
        ks = slice(j * hk, (j + 1) * hk)
        _gla_prep(q_ref[:, ks], k_ref[:, ks], lr, wg_ref[:, ks], bg_ref[:, ks], ops_ref.at[j], dec_ref.at[j])

    def mix(j):
        vs = slice(j * hv, (j + 1) * hv)
        out = _gla_mix(ops_ref.at[j], dec_ref.at[j], v_ref[:, vs], r_ref[:, vs], ng, st_ref.at[j])
        o_ref[:, vs] = out.astype(o_ref.dtype)

    prep(0)
    for j in range(nheads):
        if j + 1 < nheads:
            prep(j + 1)
        mix(j)


def _gla(p, p_lr, wg, bg, ng, tb=_GLA_TB, hps=GLA_HEADS):
    bsz, seq, _ = p.shape
    nh, hk, hv = GLA_HEADS, GLA_HK * hps, GLA_HV * hps
    dk, dv = GLA_HEADS * GLA_HK, GLA_HEADS * GLA_HV

    def spec(width, base):
        return pl.BlockSpec((None, tb, width), lambda b, h, t: (b, t, base + h))

    in_specs = [spec(hk, 0), spec(hk, dk // hk), spec(hv, 2 * dk // hv), spec(hv, (2 * dk + dv) // hv),
                pl.BlockSpec((None, tb, LANES), lambda b, h, t: (b, t, 0)),
                pl.BlockSpec((LANES, hk), lambda b, h, t: (0, h)),
                pl.BlockSpec((1, hk), lambda b, h, t: (0, h)),
                pl.BlockSpec((1, GLA_HV), lambda b, h, t: (0, 0))]
    wg_pad = jnp.pad(wg.astype(BF16), ((0, LANES - GLA_LOWRANK), (0, 0)))
    return pl.pallas_call(
        _gla_kernel, grid=(bsz, nh // hps, seq // tb), in_specs=in_specs,
        out_specs=pl.BlockSpec((None, tb, hv), lambda b, h, t: (b, t, h)),
        out_shape=jax.ShapeDtypeStruct((bsz, seq, dv), BF16),
        scratch_shapes=[pltpu.VMEM((hps, GLA_HV, GLA_HK), F32),
                        pltpu.VMEM((hps, _GLA_OPS_ROWS, GLA_HK), BF16),
                        pltpu.VMEM((hps, SUBLANES, GLA_HK), F32)],
        compiler_params=_cparams(("parallel", "parallel", "arbitrary"), 48), name="gla",
    )(p, p, p, p, p_lr, wg_pad, bg.reshape(1, dk).astype(F32), ng.reshape(1, GLA_HV).astype(F32))


def _out_proj_kernel(x_ref, y_ref, w_ref, g_ref, o_ref):
    y = jnp.dot(y_ref[...], w_ref[...], preferred_element_type=F32)
    o_ref[...] = x_ref[...] + y * _rms_scale(y) * g_ref[...]


def _out_proj(x2, y2, w_stack, wi, g_stack, gi, tm=512):
    m, d = x2.shape
    kdim = y2.shape[1]
    return pl.pallas_call(
        _out_proj_kernel, grid=(m // tm,),
        in_specs=[pl.BlockSpec((tm, d), lambda i: (i, 0)), pl.BlockSpec((tm, kdim), lambda i: (i, 0)),
                  _layer_spec(w_stack.shape[1:], wi), _layer_spec((1, d), gi)],
        out_specs=pl.BlockSpec((tm, d), lambda i: (i, 0)),
        out_shape=jax.ShapeDtypeStruct((m, d), F32),
        compiler_params=_cparams(("parallel",), 48), name="out_proj",
    )(x2, y2, w_stack, g_stack.reshape(g_stack.shape[0], 1, d))


def _even_layer(x, i, norm_pre, norm_post, w_in_b, s5p, s5_d, w_glu_b, b_glu, cw, cb, lng, lnb, w_pw_b, b_pw,
                w_out_b):
    bsz, seq, d = x.shape
    j = i // 2
    wa = s5_d.shape[0]
    a_in, rest = _norm_proj(x.reshape(bsz * seq, d), norm_pre, i, [w_in_b], j,
                            [[wa, w_in_b.shape[2] - wa]], [F32, BF16])
    gm, wz, vt, lp = _s5_prep(*s5p)
    ya = _s5_apply(a_in.reshape(bsz, seq, wa), gm, wz, vt, lp, s5_d.astype(F32))
    return _even_tail(x, ya, rest.reshape(bsz, seq, -1), i, w_glu_b, b_glu, cw, cb, lng, lnb,
                      w_pw_b, b_pw, w_out_b, norm_post)


def _odd_layer(x, i, norm_pre, norm_post, w_main_b, w_lr_b, wg, bg, ng, w_out_b):
    bsz, seq, d = x.shape
    j = i // 2
    p, p_lr = _norm_proj(x.reshape(bsz * seq, d), norm_pre, i, [w_main_b, w_lr_b], j,
                         [[w_main_b.shape[2]], [w_lr_b.shape[2]]], [BF16, BF16])
    y = _gla(p.reshape(bsz, seq, -1), p_lr.reshape(bsz, seq, -1), wg, bg, ng)
    out = _out_proj(x.reshape(bsz * seq, d), y.reshape(bsz * seq, -1), w_out_b, j, norm_post, i)
    return out.reshape(bsz, seq, d)


def kernel(x, norm_pre, norm_post, ev_w_in, s5_lambda_re, s5_lambda_im, s5_log_dt, s5_b_re, s5_b_im, s5_c_re, s5_c_im, s5_d, s5_w_glu, s5_b_glu, conv_w, conv_b, conv_ln_g, conv_ln_b, conv_w_pw, conv_b_pw, ev_w_out, od_w_in, gla_w_gate_up, gla_b_gate, gla_norm_g, od_w_out):
    depth = norm_pre.shape[0]
    norm_pre, norm_post = norm_pre.astype(F32), norm_post.astype(F32)
    ev_w_in_b, s5_w_glu_b, conv_w_pw_b = ev_w_in.astype(BF16), s5_w_glu.astype(BF16), conv_w_pw.astype(BF16)
    ev_w_out_b, od_w_out_b = ev_w_out.astype(BF16), od_w_out.astype(BF16)
    n_main = od_w_in.shape[2] - GLA_LOWRANK
    od_main_b = od_w_in[:, :, :n_main].astype(BF16)
    od_lr_b = jnp.pad(od_w_in[:, :, n_main:], ((0, 0), (0, 0), (0, LANES - GLA_LOWRANK))).astype(BF16)
    for i in range(depth):
        j = i // 2
        if i % 2 == 0:
            s5p = (s5_lambda_re[j], s5_lambda_im[j], s5_log_dt[j], s5_b_re[j], s5_b_im[j],
                   s5_c_re[j], s5_c_im[j])
            x = _even_layer(x, i, norm_pre, norm_post, ev_w_in_b, s5p, s5_d[j], s5_w_glu_b,
                            s5_b_glu[j], conv_w[j], conv_b[j], conv_ln_g[j], conv_ln_b[j],
                            conv_w_pw_b, conv_b_pw[j], ev_w_out_b)
        else:
            x = _odd_layer(x, i, norm_pre, norm_post, od_main_b, od_lr_b, gla_w_gate_up[j], gla_b_gate[j],
                           gla_norm_g[j], od_w_out_b)
    return x
```

```python
import functools

import jax
import jax.numpy as jnp
from jax import lax
from jax.experimental import pallas as pl
from jax.experimental.pallas import tpu as pltpu

F32 = jnp.float32
BF16 = jnp.bfloat16

EPS = 1e-6
LANES = 128
SUBLANES = 8

S5_GROUP = 16
S5_STATE = 64
S5_T = 16
S5_GPB = LANES // S5_GROUP
S5_SB = S5_GPB * S5_STATE
CONV_K = 31
CONV_HALO = 32
GLA_HEADS = 4
GLA_HK = 256
GLA_HV = 512
GLA_LOWRANK = 16
GLA_TAU = 16.0
GLA_CHUNK = 64

NT_DIMS = (((1,), (1,)), ((), ()))
TN_DIMS = (((0,), (0,)), ((), ()))


def _cparams(sem, vmem_mb):
    return pltpu.CompilerParams(dimension_semantics=sem, vmem_limit_bytes=vmem_mb * 1024 * 1024)


def _const_spec(shape):
    nd = len(shape)
    return pl.BlockSpec(shape, lambda *_: (0,) * nd, pipeline_mode=pl.Buffered(1))


def _rms_scale(x):
    return lax.rsqrt(jnp.mean(x * x, axis=-1, keepdims=True) + EPS)


def _sigmoid(x):
    return 0.5 * jnp.tanh(0.5 * x) + 0.5


def _silu(x):
    return x * _sigmoid(x)


def _cast_kernel(w_ref, o_ref):
    o_ref[...] = w_ref[...].astype(o_ref.dtype)


def _cast_cols(w, n_cols, bn=512):
    nl, kdim, _ = w.shape
    spec = pl.BlockSpec((None, kdim, bn), lambda l, c: (l, 0, c))
    return pl.pallas_call(
        _cast_kernel, grid=(nl, n_cols // bn), in_specs=[spec], out_specs=spec,
        out_shape=jax.ShapeDtypeStruct((nl, kdim, n_cols), BF16),
        compiler_params=_cparams(("parallel", "parallel"), 32), name="cast_cols",
    )(w)


def _layer_spec(shape, layer):
    nd = len(shape)
    return pl.BlockSpec((None,) + tuple(shape), lambda *_: (layer,) + (0,) * nd, pipeline_mode=pl.Buffered(1))


def _norm_proj_kernel(x_ref, g_ref, *refs, n_w, splits, n_chunk):
    w_refs, o_refs = refs[:n_w], refs[n_w:]
    x = x_ref[...]
    u = (x * _rms_scale(x) * g_ref[...]).astype(BF16)
    outs = iter(o_refs)
    for w_ref, widths in zip(w_refs, splits):
        base = 0
        for width in widths:
            o_ref = next(outs)
            for n0 in range(0, width, n_chunk):
                n1 = min(n0 + n_chunk, width)
                o_ref[:, n0:n1] = jnp.dot(u, w_ref[:, base + n0:base + n1],
                                          preferred_element_type=F32).astype(o_ref.dtype)
            base += width


def _norm_proj(x2, g_stack, gi, w_stacks, wi, splits, out_dtypes, tm=512, n_chunk=512):
    m, d = x2.shape
    widths = [wd for ws in splits for wd in ws]
    in_specs = [pl.BlockSpec((tm, d), lambda i: (i, 0)), _layer_spec((1, d), gi)]
    in_specs += [_layer_spec(w.shape[1:], wi) for w in w_stacks]
    out_specs = [pl.BlockSpec((tm, wd), lambda i: (i, 0)) for wd in widths]
    out_shape = [jax.ShapeDtypeStruct((m, wd), dt) for wd, dt in zip(widths, out_dtypes)]
    return pl.pallas_call(
        functools.partial(_norm_proj_kernel, n_w=len(w_stacks), splits=splits, n_chunk=n_chunk),
        grid=(m // tm,), in_specs=in_specs, out_specs=out_specs, out_shape=out_shape,
        compiler_params=_cparams(("parallel",), 56), name="norm_proj",
    )(x2, g_stack.reshape(g_stack.shape[0], 1, d), *w_stacks)


def _even_in_kernel(x_ref, g_ref, w_ref, a_ref, gates_ref, *, n_chunk):
    wa = a_ref.shape[1]
    x = x_ref[...]
    u = (x * _rms_scale(x) * g_ref[...]).astype(BF16)

    def proj(part, n0):
        c0 = part * wa + n0
        return jnp.dot(u, w_ref[:, c0:c0 + n_chunk], preferred_element_type=F32)

    for n0 in range(0, wa, n_chunk):
        cols = slice(n0, n0 + n_chunk)
        a_ref[:, cols] = proj(0, n0)
        gates_ref[:, cols] = _silu(proj(1, n0)).astype(gates_ref.dtype)
        gates_ref[:, wa + n0:wa + n0 + n_chunk] = (proj(2, n0) * _sigmoid(proj(3, n0))).astype(gates_ref.dtype)
        gates_ref[:, 2 * wa + n0:2 * wa + n0 + n_chunk] = _silu(proj(4, n0)).astype(gates_ref.dtype)


def _even_in(x2, g_stack, gi, w_stack, wi, wa, tm=512, n_chunk=512):
    m, d = x2.shape
    assert w_stack.shape[2] == 5 * wa and wa % n_chunk == 0
    return pl.pallas_call(
        functools.partial(_even_in_kernel, n_chunk=n_chunk), grid=(m // tm,),
        in_specs=[pl.BlockSpec((tm, d), lambda i: (i, 0)), _layer_spec((1, d), gi),
                  _layer_spec(w_stack.shape[1:], wi)],
        out_specs=[pl.BlockSpec((tm, wa), lambda i: (i, 0)), pl.BlockSpec((tm, 3 * wa), lambda i: (i, 0))],
        out_shape=[jax.ShapeDtypeStruct((m, wa), F32), jax.ShapeDtypeStruct((m, 3 * wa), BF16)],
        compiler_params=_cparams(("parallel",), 56), name="even_in",
    )(x2, g_stack.reshape(g_stack.shape[0], 1, d), w_stack)


def _s5_prep_kernel(lre_ref, lim_ref, dt_ref, btr_ref, bti_ref, cr_ref, ci_ref,
                    g_ref, wz_ref, vt_ref, lp_ref):
    lre, lim, dt = lre_ref[...], lim_ref[...], jnp.exp(dt_ref[...])

    def lam_pow(m):
        mag = jnp.exp(lre * dt * float(m))
        ang = lim * dt * float(m)
        return mag * jnp.cos(ang), mag * jnp.sin(ang)

    pows = [lam_pow(m) for m in range(S5_T + 1)]
    nr, ni = pows[1][0] - 1.0, pows[1][1]
    den = lre * lre + lim * lim
    cfr = (nr * lre + ni * lim) / den
    cfi = (ni * lre - nr * lim) / den
    btr, bti = btr_ref[...], bti_ref[...]
    bbr = cfr * btr - cfi * bti
    bbi = cfr * bti + cfi * btr
    cr, ci = cr_ref[...], ci_ref[...]

    for j in range(S5_T):
        pr, pi = pows[S5_T - 1 - j]
        wz_ref[j * LANES:(j + 1) * LANES, 0:S5_SB] = (pr * bbr - pi * bbi).astype(wz_ref.dtype)
        wz_ref[j * LANES:(j + 1) * LANES, S5_SB:] = (pr * bbi + pi * bbr).astype(wz_ref.dtype)
    for i in range(S5_T):
        pr, pi = pows[i + 1]
        vt_ref[i * LANES:(i + 1) * LANES, 0:S5_SB] = (cr * pr - ci * pi).astype(vt_ref.dtype)
        vt_ref[i * LANES:(i + 1) * LANES, S5_SB:] = (-(cr * pi + ci * pr)).astype(vt_ref.dtype)
    def split(t):
        hi = t.astype(BF16)
        return hi, (t - hi.astype(F32)).astype(BF16)

    def dot_nt(a, b):
        return lax.dot_general(a, b, NT_DIMS, preferred_element_type=F32)

    bb_hi, bb_lo = split(jnp.concatenate([bbr, bbi], axis=1))
    kts = []
    for tau in range(S5_T):
        pr, pi = pows[tau]
        ct_hi, ct_lo = split(jnp.concatenate([cr * pr - ci * pi, -(cr * pi + ci * pr)], axis=1))
        kts.append(dot_nt(bb_hi, ct_hi) + dot_nt(bb_hi, ct_lo) + dot_nt(bb_lo, ct_hi))
    zero = jnp.zeros((LANES, LANES), F32)
    for r in range(S5_T):
        for e in range(2):
            tau = S5_T - 2 - r + e
            kt = kts[tau] if tau >= 0 else zero
            g_ref[r * LANES:(r + 1) * LANES, e * LANES:(e + 1) * LANES] = kt.astype(g_ref.dtype)
    for r in range(2 * SUBLANES):
        pr, pi = lam_pow(S5_T * r) if r <= SUBLANES else (jnp.zeros_like(lre), jnp.zeros_like(lre))
        lp_ref[r:r + 1, 0:S5_SB] = pr
        lp_ref[r:r + 1, S5_SB:] = pi


def _s5_prep(lam_re, lam_im, log_dt, b_re, b_im, c_re, c_im):
    ng = lam_re.shape[0]
    nblk = ng // S5_GPB
    eye = jnp.eye(S5_GPB, dtype=F32)

    def rows(t):
        return t.astype(F32).reshape(nblk, 1, S5_SB)

    def embed(t):
        t = t.astype(F32).reshape(nblk, S5_GPB, S5_GROUP, S5_STATE)
        t = t[:, :, :, None, :] * eye[None, :, None, :, None]
        return t.reshape(nblk, LANES, S5_SB)

    dt_rows = jnp.broadcast_to(log_dt.astype(F32)[:, None], (ng, S5_STATE))
    ins = [rows(lam_re), rows(lam_im), rows(dt_rows),
           embed(jnp.swapaxes(b_re, 1, 2)), embed(jnp.swapaxes(b_im, 1, 2)),
           embed(c_re), embed(c_im)]
    row_spec = pl.BlockSpec((None, 1, S5_SB), lambda k: (k, 0, 0))
    mat_spec = pl.BlockSpec((None, LANES, S5_SB), lambda k: (k, 0, 0))
    kt = S5_T * LANES
    out_shape = [jax.ShapeDtypeStruct((nblk, kt, 2 * LANES), BF16),
                 jax.ShapeDtypeStruct((nblk, kt, 2 * S5_SB), BF16),
                 jax.ShapeDtypeStruct((nblk, kt, 2 * S5_SB), BF16),
                 jax.ShapeDtypeStruct((nblk, 2 * SUBLANES, 2 * S5_SB), F32)]
    out_specs = [pl.BlockSpec((None,) + s.shape[1:], lambda k: (k, 0, 0)) for s in out_shape]
    return pl.pallas_call(
        _s5_prep_kernel, grid=(nblk,),
        in_specs=[row_spec] * 3 + [mat_spec] * 4, out_specs=out_specs, out_shape=out_shape,
        compiler_params=_cparams(("parallel",), 48), name="s5_prep",
    )(*ins)


def _s5_kernel(u_ref, g_ref, wz_ref, vt_ref, lp_ref, d_ref, o_ref, u16_ref, z_ref, s_ref, y_ref):
    nc = u16_ref.shape[0]
    kq = S5_T * LANES // 4
    z = None
    for q in range(4):
        for j in range(q * S5_T // 4, (q + 1) * S5_T // 4):
            u16_ref[:, j * LANES:(j + 1) * LANES] = u_ref[pl.ds(j, nc, stride=S5_T), :].astype(BF16)
        zq = jnp.dot(u16_ref[:, q * kq:(q + 1) * kq], wz_ref[q * kq:(q + 1) * kq, :],
                     preferred_element_type=F32)
        z = zq if z is None else z + zq
    z_ref[...] = z

    row = lax.broadcasted_iota(jnp.int32, (SUBLANES, S5_SB), 0)

    def shift(x, sh):
        return jnp.where(row >= sh, pltpu.roll(x, sh, axis=0), 0.0)

    def lp(r0, r1):
        return lp_ref[r0:r1, 0:S5_SB], lp_ref[r0:r1, S5_SB:]

    def scan_rows(r0, carry):
        cr, ci = carry
        xr = z_ref[r0:r0 + SUBLANES, 0:S5_SB]
        xi = z_ref[r0:r0 + SUBLANES, S5_SB:]
        for sh in (1, 2, 4):
            pr, pi = lp(sh, sh + 1)
            sr, si = shift(xr, sh), shift(xi, sh)
            xr, xi = xr + pr * sr - pi * si, xi + pr * si + pi * sr
        pr, pi = lp(0, SUBLANES)
        s_ref[r0:r0 + SUBLANES, 0:S5_SB] = (pr * cr - pi * ci + shift(xr, 1)).astype(s_ref.dtype)
        s_ref[r0:r0 + SUBLANES, S5_SB:] = (pr * ci + pi * cr + shift(xi, 1)).astype(s_ref.dtype)
        pr, pi = lp(SUBLANES, SUBLANES + 1)
        last = SUBLANES - 1
        return (pr * cr - pi * ci + xr[last:last + 1], pr * ci + pi * cr + xi[last:last + 1])

    pairs = S5_T // 2
    for ip in range(pairs):
        kk = (2 * ip + 2) * LANES
        y_ref[:, 2 * ip * LANES:(2 * ip + 2) * LANES] = jnp.dot(
            u16_ref[:, 0:kk], g_ref[(S5_T - 2 - 2 * ip) * LANES:, :], preferred_element_type=F32)

    carry = (jnp.zeros((1, S5_SB), F32), jnp.zeros((1, S5_SB), F32))
    for r0 in range(0, nc, SUBLANES):
        carry = scan_rows(r0, carry)

    sb = s_ref[...].astype(BF16)
    d = d_ref[...]
    for ip in range(pairs):
        acc = y_ref[:, 2 * ip * LANES:(2 * ip + 2) * LANES] + lax.dot_general(
            sb, vt_ref[2 * ip * LANES:(2 * ip + 2) * LANES, :], NT_DIMS, preferred_element_type=F32)
        for e in range(2):
            i = 2 * ip + e
            ui = u_ref[pl.ds(i, nc, stride=S5_T), :]
            o_ref[pl.ds(i, nc, stride=S5_T), :] = acc[:, e * LANES:(e + 1) * LANES] + d * ui


def _s5_apply(a_in, gm, wz, vt, lp, d):
    bsz, seq, width = a_in.shape
    nblk = width // LANES
    nc = seq // S5_T
    kt = S5_T * LANES
    act_spec = pl.BlockSpec((None, seq, LANES), lambda k, b: (b, 0, k))

    def wspec(shape):
        return pl.BlockSpec((None,) + shape, lambda k, b: (k, 0, 0))

    return pl.pallas_call(
        _s5_kernel, grid=(nblk, bsz),
        in_specs=[act_spec, wspec((kt, 2 * LANES)), wspec((kt, 2 * S5_SB)), wspec((kt, 2 * S5_SB)),
                  wspec((2 * SUBLANES, 2 * S5_SB)), pl.BlockSpec((1, LANES), lambda k, b: (0, k))],
        out_specs=act_spec,
        out_shape=jax.ShapeDtypeStruct((bsz, seq, width), F32),
        scratch_shapes=[pltpu.VMEM((nc, kt), BF16), pltpu.VMEM((nc, 2 * S5_SB), F32),
                        pltpu.VMEM((nc, 2 * S5_SB), F32), pltpu.VMEM((nc, kt), F32)],
        compiler_params=_cparams(("parallel", "parallel"), 56), name="s5_apply",
    )(a_in, gm, wz, vt, lp, d.reshape(1, width))


def _even_tail_kernel(x_ref, ya_ref, sa_ref, hin_ref, sb_ref,
                      wglu_ref, bglu_ref, cw_ref, cb_ref, lng_ref, lnb_ref,
                      wpw_ref, bpw_ref, wout_ref, gpost_ref, o_ref, h_ref, hs_ref, c_ref, *, rows):
    tm, w = c_ref.shape
    nsh = hs_ref.shape[1]

    @pl.when(pl.program_id(1) == 0)
    def _():
        h_ref[0:CONV_HALO, :] = jnp.zeros((CONV_HALO, w), F32)

    h_ref[CONV_HALO:, :] = hin_ref[...].astype(F32)
    for s in range(1, SUBLANES):
        hs_ref[s - 1] = h_ref[s:s + nsh, :]

    off = CONV_HALO - (CONV_K - 1)

    def conv_rows(ri, _):
        r0 = pl.multiple_of(ri * rows, rows)
        for c0 in range(0, w, LANES):
            acc = jnp.broadcast_to(cb_ref[:, c0:c0 + LANES], (rows, LANES))
            for s in range(SUBLANES):
                taps = [o for o in range(off, off + CONV_K) if o % SUBLANES == s]
                src = h_ref if s == 0 else hs_ref.at[s - 1]
                lo, hi = taps[0] - s, taps[-1] - s + rows
                win = src[pl.ds(pl.multiple_of(r0 + lo, SUBLANES), hi - lo), c0:c0 + LANES]
                for o in taps:
                    k = o - off
                    acc = acc + cw_ref[k:k + 1, c0:c0 + LANES] * win[o - s - lo:o - s - lo + rows]
            c_ref[pl.ds(r0, rows), c0:c0 + LANES] = acc
        return 0

    lax.fori_loop(0, tm // rows, conv_rows, 0)
    h_ref[0:CONV_HALO, :] = h_ref[tm:tm + CONV_HALO, :]

    hc = c_ref[...]
    mu = jnp.mean(hc, axis=-1, keepdims=True)
    xc = hc - mu
    hn = xc * lax.rsqrt(jnp.mean(xc * xc, axis=-1, keepdims=True) + EPS) * lng_ref[...] + lnb_ref[...]
    yb = jnp.dot(_silu(hn).astype(BF16), wpw_ref[...], preferred_element_type=F32) + bpw_ref[...]
    yb = yb * sb_ref[...].astype(F32)

    ya = jax.nn.gelu(ya_ref[...])
    gate = jnp.dot(ya.astype(BF16), wglu_ref[...], preferred_element_type=F32) + bglu_ref[...]
    ya = ya * _sigmoid(gate) * sa_ref[...].astype(F32)

    y = jnp.dot(ya.astype(BF16), wout_ref[0:w, :], preferred_element_type=F32)
    y = y + jnp.dot(yb.astype(BF16), wout_ref[w:, :], preferred_element_type=F32)
    o_ref[...] = x_ref[...] + y * _rms_scale(y) * gpost_ref[...]


def _even_tail(x, ya, p, i, wglu_b, bglu, cw, cb, lng, lnb, wpw_b, bpw, wout_b, norm_post, tm=512, rows=32):
    bsz, seq, d = x.shape
    w = ya.shape[-1]
    j = i // 2

    def act(col):
        return pl.BlockSpec((None, tm, w), lambda b, t: (b, t, col))

    vec = lambda v: v.reshape(1, -1).astype(F32)
    cw_pad = jnp.pad(cw.astype(F32), ((0, CONV_HALO - CONV_K), (0, 0)))
    consts = [wglu_b, vec(bglu), cw_pad, vec(cb), vec(lng), vec(lnb), wpw_b, vec(bpw), wout_b,
              norm_post.reshape(norm_post.shape[0], 1, d)]
    stacked = {0: j, 6: j, 8: j, 9: i}
    const_specs = [_layer_spec(c.shape[1:], stacked[n]) if n in stacked else _const_spec(c.shape)
                   for n, c in enumerate(consts)]
    return pl.pallas_call(
        functools.partial(_even_tail_kernel, rows=rows), grid=(bsz, seq // tm),
        in_specs=[act(0), act(0), act(0), act(1), act(2)] + const_specs,
        out_specs=act(0),
        out_shape=jax.ShapeDtypeStruct((bsz, seq, d), F32),
        scratch_shapes=[pltpu.VMEM((CONV_HALO + tm, w), F32),
                        pltpu.VMEM((SUBLANES - 1, CONV_HALO + tm - SUBLANES, w), F32),
                        pltpu.VMEM((tm, w), F32)],
        compiler_params=_cparams(("parallel", "arbitrary"), 56), name="even_tail",
    )(x, ya, p, p, p, *consts)


_GLA_C = GLA_CHUNK
_GLA_TB = 4 * _GLA_C
_R_QD, _R_KD, _R_KU, _R_QX, _R_KX = (i * _GLA_TB for i in range(5))
_R_QHI, _R_KLO, _R_KB0, _R_KB1 = (5 * _GLA_TB + i * 2 * _GLA_C for i in range(4))
_GLA_OPS_ROWS = 7 * _GLA_TB


def _gla_prep(q_b, k_b, lr, wg, bg, ops_ref, dec_ref):
    tb, hk = q_b.shape
    c = _GLA_C
    assert tb == _GLA_TB
    tpc = c // SUBLANES

    gp = jnp.dot(lr, wg, preferred_element_type=F32) + bg
    g = (jnp.minimum(gp, 0.0) - jnp.log(1.0 + jnp.exp(-jnp.abs(gp)))) * (1.0 / GLA_TAU)

    row8 = lax.broadcasted_iota(jnp.int32, (SUBLANES, hk), 0)
    tiles = []
    for i in range(tb // SUBLANES):
        x = g[i * SUBLANES:(i + 1) * SUBLANES]
        for sh in (1, 2, 4):
            x = x + jnp.where(row8 >= sh, pltpu.roll(x, sh, axis=0), 0.0)
        if i % tpc:
            x = x + tiles[-1][SUBLANES - 1:SUBLANES, :]
        tiles.append(x)
    bc = jnp.concatenate(tiles, axis=0)
    bl = [tiles[(a + 1) * tpc - 1][SUBLANES - 1:SUBLANES, :] for a in range(tb // c)]
    bl_rows = jnp.concatenate([jnp.broadcast_to(b, (c, hk)) for b in bl], axis=0)

    q = q_b.astype(F32) * (GLA_HK ** -0.5)
    k = k_b.astype(F32)
    qd = q * jnp.exp(bc)
    kd = k * jnp.exp(-bc)
    ku = k * jnp.exp(bl_rows - bc)
    qd_b, kd_b, ku_b = qd.astype(BF16), kd.astype(BF16), ku.astype(BF16)
    ops_ref[_R_QD:_R_QD + tb] = qd_b
    ops_ref[_R_KD:_R_KD + tb] = kd_b
    ops_ref[_R_KU:_R_KU + tb] = ku_b

    def rows(t, a):
        return t[a * c:(a + 1) * c]

    def put(r0, a, val):
        ops_ref[r0 + a * c:r0 + (a + 1) * c] = val

    e0, e1, e2, e3 = (jnp.exp(b) for b in bl)
    e01 = jnp.exp(bl[0] + bl[1])
    e012 = jnp.exp(bl[0] + bl[1] + bl[2])
    e23 = jnp.exp(bl[2] + bl[3])
    e123 = jnp.exp(bl[1] + bl[2] + bl[3])
    dec_ref[...] = jnp.broadcast_to(jnp.exp(bl[0] + bl[1] + bl[2] + bl[3]), dec_ref.shape)
    put(_R_QX, 0, rows(qd_b, 0))
    put(_R_QX, 1, (rows(qd, 1) * e0).astype(BF16))
    put(_R_QX, 2, (rows(qd, 2) * e01).astype(BF16))
    put(_R_QX, 3, (rows(qd, 3) * e012).astype(BF16))
    put(_R_KX, 0, (rows(ku, 0) * e123).astype(BF16))
    put(_R_KX, 1, (rows(ku, 1) * e23).astype(BF16))
    put(_R_KX, 2, (rows(ku, 2) * e3).astype(BF16))
    put(_R_KX, 3, rows(ku_b, 3))
    put(_R_QHI, 0, rows(qd_b, 2))
    put(_R_QHI, 1, (rows(qd, 3) * e2).astype(BF16))
    put(_R_KLO, 0, (rows(ku, 0) * e1).astype(BF16))
    put(_R_KLO, 1, rows(ku_b, 1))
    put(_R_KB0, 0, rows(ku_b, 0))
    put(_R_KB0, 1, rows(kd_b, 1))
    put(_R_KB1, 0, rows(ku_b, 2))
    put(_R_KB1, 1, rows(kd_b, 3))


def _gla_mix(ops_ref, dec_ref, v, r_b, ng, st_ref):
    c = _GLA_C
    half = 2 * c

    def op(r0, n, a=0):
        return ops_ref[r0 + a * c:r0 + (a + n) * c]

    cross = lax.dot_general(op(_R_QHI, 2), op(_R_KLO, 2), NT_DIMS, preferred_element_type=F32).astype(BF16)
    ri = lax.broadcasted_iota(jnp.int32, (half, half), 0)
    ci = lax.broadcasted_iota(jnp.int32, (half, half), 1)
    causal = ri >= ci

    def half_attn(h):
        top = lax.dot_general(op(_R_QD, 1, 2 * h), op(_R_KD, 2, 2 * h), NT_DIMS, preferred_element_type=F32)
        bot = lax.dot_general(op(_R_QD, 1, 2 * h + 1), op(_R_KB1 if h else _R_KB0, 2), NT_DIMS,
                              preferred_element_type=F32)
        return jnp.where(causal, jnp.concatenate([top, bot], axis=0), 0.0).astype(BF16)

    st = st_ref[...]
    o_top = jnp.dot(half_attn(0), v[0:half], preferred_element_type=F32)
    o_bot = jnp.dot(jnp.concatenate([cross, half_attn(1)], axis=1), v, preferred_element_type=F32)
    o = jnp.concatenate([o_top, o_bot], axis=0)
    o = o + lax.dot_general(op(_R_QX, 4), st.astype(BF16), NT_DIMS, preferred_element_type=F32)
    st_ref[...] = st * dec_ref[0:1, :] + lax.dot_general(v, op(_R_KX, 4), TN_DIMS, preferred_element_type=F32)
    on = o * _rms_scale(o) * ng
    return on * _silu(r_b.astype(F32))


def _gla_kernel(x_ref, q_ref, k_ref, v_ref, r_ref, lr_ref, wg_ref, bg_ref, ng_ref, wout_ref, gpost_ref,
                o_ref, st_ref, ops_ref, dec_ref):
    @pl.when(pl.program_id(1) == 0)
    def _():
        st_ref[...] = jnp.zeros_like(st_ref)

    hk, hv = GLA_HK, GLA_HV
    nheads = st_ref.shape[0]
    lr, ng = lr_ref[...], ng_ref[...]

    def prep(j):
        ks = slice(j * hk, (j + 1) * hk)
        _gla_prep(q_ref[:, ks], k_ref[:, ks], lr, wg_ref[:, ks], bg_ref[:, ks], ops_ref.at[j], dec_ref.at[j])

    def mix(j):
        vs = slice(j * hv, (j + 1) * hv)
        out = _gla_mix(ops_ref.at[j], dec_ref.at[j], v_ref[:, vs], r_ref[:, vs], ng, st_ref.at[j])
        return jnp.dot(out.astype(BF16), wout_ref[vs, :], preferred_element_type=F32)

    prep(0)
    y = None
    for j in range(nheads):
        if j + 1 < nheads:
            prep(j + 1)
        part = mix(j)
        y = part if y is None else y + part
    o_ref[...] = x_ref[...] + y * _rms_scale(y) * gpost_ref[...]


def _gla_out(x, p, p_lr, wg, bg, ng, w_out_b, wi, norm_post, gi, tb=_GLA_TB):
    bsz, seq, d = x.shape
    nh = GLA_HEADS
    dk, dv = nh * GLA_HK, nh * GLA_HV

    def act(width, col):
        return pl.BlockSpec((None, tb, width), lambda b, t: (b, t, col))

    in_specs = [act(d, 0), act(dk, 0), act(dk, 1), act(dv, 2 * dk // dv), act(dv, (2 * dk + dv) // dv),
                act(LANES, 0), _const_spec((LANES, dk)), _const_spec((1, dk)), _const_spec((1, GLA_HV)),
                _layer_spec(w_out_b.shape[1:], wi), _layer_spec((1, d), gi)]
    wg_pad = jnp.pad(wg.astype(BF16), ((0, LANES - GLA_LOWRANK), (0, 0)))
    return pl.pallas_call(
        _gla_kernel, grid=(bsz, seq // tb), in_specs=in_specs,
        out_specs=act(d, 0),
        out_shape=jax.ShapeDtypeStruct((bsz, seq, d), F32),
        scratch_shapes=[pltpu.VMEM((nh, GLA_HV, GLA_HK), F32),
                        pltpu.VMEM((nh, _GLA_OPS_ROWS, GLA_HK), BF16),
                        pltpu.VMEM((nh, SUBLANES, GLA_HK), F32)],
        compiler_params=_cparams(("parallel", "arbitrary"), 48), name="gla_out",
    )(x, p, p, p, p, p_lr, wg_pad, bg.reshape(1, dk).astype(F32), ng.reshape(1, GLA_HV).astype(F32),
      w_out_b, norm_post.reshape(norm_post.shape[0], 1, d))


def _even_layer(x, i, norm_pre, norm_post, w_in_b, s5p, s5_d, w_glu_b, b_glu, cw, cb, lng, lnb, w_pw_b, b_pw,
                w_out_b):
    bsz, seq, d = x.shape
    j = i // 2
    wa = s5_d.shape[0]
    a_in, gates = _even_in(x.reshape(bsz * seq, d), norm_pre, i, w_in_b, j, wa)
    gm, wz, vt, lp = _s5_prep(*s5p)
    ya = _s5_apply(a_in.reshape(bsz, seq, wa), gm, wz, vt, lp, s5_d.astype(F32))
    return _even_tail(x, ya, gates.reshape(bsz, seq, -1), i, w_glu_b, b_glu, cw, cb, lng, lnb,
                      w_pw_b, b_pw, w_out_b, norm_post)


def _odd_layer(x, i, norm_pre, norm_post, w_main_b, w_lr_b, wg, bg, ng, w_out_b):
    bsz, seq, d = x.shape
    j = i // 2
    p, p_lr = _norm_proj(x.reshape(bsz * seq, d), norm_pre, i, [w_main_b, w_lr_b], j,
                         [[w_main_b.shape[2]], [w_lr_b.shape[2]]], [BF16, BF16])
    return _gla_out(x, p.reshape(bsz, seq, -1), p_lr.reshape(bsz, seq, -1), wg, bg, ng, w_out_b, j, norm_post, i)


def kernel(x, norm_pre, norm_post, ev_w_in, s5_lambda_re, s5_lambda_im, s5_log_dt, s5_b_re, s5_b_im, s5_c_re, s5_c_im, s5_d, s5_w_glu, s5_b_glu, conv_w, conv_b, conv_ln_g, conv_ln_b, conv_w_pw, conv_b_pw, ev_w_out, od_w_in, gla_w_gate_up, gla_b_gate, gla_norm_g, od_w_out):
    depth = norm_pre.shape[0]
    norm_pre, norm_post = norm_pre.astype(F32), norm_post.astype(F32)
    ev_w_in_b, s5_w_glu_b, conv_w_pw_b = ev_w_in.astype(BF16), s5_w_glu.astype(BF16), conv_w_pw.astype(BF16)
    ev_w_out_b, od_w_out_b = ev_w_out.astype(BF16), od_w_out.astype(BF16)
    n_main = od_w_in.shape[2] - GLA_LOWRANK
    od_main_b = _cast_cols(od_w_in, n_main)
    od_lr_b = jnp.pad(od_w_in[:, :, n_main:], ((0, 0), (0, 0), (0, LANES - GLA_LOWRANK))).astype(BF16)
    for i in range(depth):
        j = i // 2
        if i % 2 == 0:
            s5p = (s5_lambda_re[j], s5_lambda_im[j], s5_log_dt[j], s5_b_re[j], s5_b_im[j],
                   s5_c_re[j], s5_c_im[j])
            x = _even_layer(x, i, norm_pre, norm_post, ev_w_in_b, s5p, s5_d[j], s5_w_glu_b,
                            s5_b_glu[j], conv_w[j], conv_b[j], conv_ln_g[j], conv_ln_b[j],
                            conv_w_pw_b, conv_b_pw[j], ev_w_out_b)
        else:
            x = _odd_layer(x, i, norm_pre, norm_post, od_main_b, od_lr_b, gla_w_gate_up[j], gla_b_gate[j],
                           gla_norm_g[j], od_w_out_b)
    return x
```

```python
import functools

import jax
import jax.numpy as jnp
from jax import lax
from jax.experimental import pallas as pl
from jax.experimental.pallas import tpu as pltpu

F32 = jnp.float32
BF16 = jnp.bfloat16

EPS = 1e-6
LANES = 128
SUBLANES = 8

S5_GROUP = 16
S5_STATE = 64
S5_T = 16
S5_GPB = LANES // S5_GROUP
S5_SB = S5_GPB * S5_STATE
CONV_K = 31
CONV_HALO = 32
GLA_HEADS = 4
GLA_HK = 256
GLA_HV = 512
GLA_LOWRANK = 16
GLA_TAU = 16.0
GLA_CHUNK = 64

NT_DIMS = (((1,), (1,)), ((), ()))
TN_DIMS = (((0,), (0,)), ((), ()))


def _cparams(sem, vmem_mb):
    return pltpu.CompilerParams(dimension_semantics=sem, vmem_limit_bytes=vmem_mb * 1024 * 1024)


def _const_spec(shape):
    nd = len(shape)
    return pl.BlockSpec(shape, lambda *_: (0,) * nd, pipeline_mode=pl.Buffered(1))


def _rms_scale(x):
    return lax.rsqrt(jnp.mean(x * x, axis=-1, keepdims=True) + EPS)


def _sigmoid(x):
    return 0.5 * jnp.tanh(0.5 * x) + 0.5


def _silu(x):
    return x * _sigmoid(x)


def _layer_spec(shape, layer):
    nd = len(shape)
    return pl.BlockSpec((None,) + tuple(shape), lambda *_: (layer,) + (0,) * nd, pipeline_mode=pl.Buffered(1))


def _norm_proj_kernel(x_ref, g_ref, *refs, n_w, splits, n_chunk):
    w_refs, o_refs = refs[:n_w], refs[n_w:]
    x = x_ref[...]
    u = (x * _rms_scale(x) * g_ref[...]).astype(BF16)
    outs = iter(o_refs)
    for w_ref, widths in zip(w_refs, splits):
        base = 0
        for width in widths:
            o_ref = next(outs)
            for n0 in range(0, width, n_chunk):
                n1 = min(n0 + n_chunk, width)
                o_ref[:, n0:n1] = jnp.dot(u, w_ref[:, base + n0:base + n1],
                                          preferred_element_type=F32).astype(o_ref.dtype)
            base += width


def _norm_proj(x2, g_stack, gi, w_stacks, wi, splits, out_dtypes, tm=512, n_chunk=512):
    m, d = x2.shape
    widths = [wd for ws in splits for wd in ws]
    in_specs = [pl.BlockSpec((tm, d), lambda i: (i, 0)), _layer_spec((1, d), gi)]
    in_specs += [_layer_spec(w.shape[1:], wi) for w in w_stacks]
    out_specs = [pl.BlockSpec((tm, wd), lambda i: (i, 0)) for wd in widths]
    out_shape = [jax.ShapeDtypeStruct((m, wd), dt) for wd, dt in zip(widths, out_dtypes)]
    return pl.pallas_call(
        functools.partial(_norm_proj_kernel, n_w=len(w_stacks), splits=splits, n_chunk=n_chunk),
        grid=(m // tm,), in_specs=in_specs, out_specs=out_specs, out_shape=out_shape,
        compiler_params=_cparams(("parallel",), 56), name="norm_proj",
    )(x2, g_stack.reshape(g_stack.shape[0], 1, d), *w_stacks)


def _even_in_kernel(x_ref, g_ref, w_ref, a_ref, gates_ref, *, rows, n_chunk):
    tm, wa = a_ref.shape
    for r0 in range(0, tm, rows):
        rs = slice(r0, r0 + rows)
        x = x_ref[rs, :]
        u = (x * _rms_scale(x) * g_ref[...]).astype(BF16)

        def proj(part, n0):
            c0 = part * wa + n0
            return jnp.dot(u, w_ref[:, c0:c0 + n_chunk], preferred_element_type=F32)

        for n0 in range(0, wa, n_chunk):
            cols = slice(n0, n0 + n_chunk)
            a_ref[rs, cols] = proj(0, n0)
            gates_ref[rs, cols] = _silu(proj(1, n0)).astype(gates_ref.dtype)
            gates_ref[rs, wa + n0:wa + n0 + n_chunk] = (
                proj(2, n0) * _sigmoid(proj(3, n0))).astype(gates_ref.dtype)
            gates_ref[rs, 2 * wa + n0:2 * wa + n0 + n_chunk] = _silu(proj(4, n0)).astype(gates_ref.dtype)


def _even_in(x2, g_stack, gi, w_stack, wi, wa, tm=1024, rows=512, n_chunk=512):
    m, d = x2.shape
    assert w_stack.shape[2] == 5 * wa and wa % n_chunk == 0
    return pl.pallas_call(
        functools.partial(_even_in_kernel, rows=rows, n_chunk=n_chunk), grid=(m // tm,),
        in_specs=[pl.BlockSpec((tm, d), lambda i: (i, 0)), _layer_spec((1, d), gi),
                  _layer_spec(w_stack.shape[1:], wi)],
        out_specs=[pl.BlockSpec((tm, wa), lambda i: (i, 0)), pl.BlockSpec((tm, 3 * wa), lambda i: (i, 0))],
        out_shape=[jax.ShapeDtypeStruct((m, wa), F32), jax.ShapeDtypeStruct((m, 3 * wa), BF16)],
        compiler_params=_cparams(("parallel",), 56), name="even_in",
    )(x2, g_stack.reshape(g_stack.shape[0], 1, d), w_stack)


def _s5_prep_kernel(lre_ref, lim_ref, dt_ref, btr_ref, bti_ref, cr_ref, ci_ref,
                    g_ref, wz_ref, vt_ref, lp_ref):
    lre, lim, dt = lre_ref[...], lim_ref[...], jnp.exp(dt_ref[...])

    def lam_pow(m):
        mag = jnp.exp(lre * dt * float(m))
        ang = lim * dt * float(m)
        return mag * jnp.cos(ang), mag * jnp.sin(ang)

    pows = [lam_pow(m) for m in range(S5_T + 1)]
    nr, ni = pows[1][0] - 1.0, pows[1][1]
    den = lre * lre + lim * lim
    cfr = (nr * lre + ni * lim) / den
    cfi = (ni * lre - nr * lim) / den
    btr, bti = btr_ref[...], bti_ref[...]
    bbr = cfr * btr - cfi * bti
    bbi = cfr * bti + cfi * btr
    cr, ci = cr_ref[...], ci_ref[...]

    for j in range(S5_T):
        pr, pi = pows[S5_T - 1 - j]
        wz_ref[j * LANES:(j + 1) * LANES, 0:S5_SB] = (pr * bbr - pi * bbi).astype(wz_ref.dtype)
        wz_ref[j * LANES:(j + 1) * LANES, S5_SB:] = (pr * bbi + pi * bbr).astype(wz_ref.dtype)
    for i in range(S5_T):
        pr, pi = pows[i + 1]
        vt_ref[i * LANES:(i + 1) * LANES, 0:S5_SB] = (cr * pr - ci * pi).astype(vt_ref.dtype)
        vt_ref[i * LANES:(i + 1) * LANES, S5_SB:] = (-(cr * pi + ci * pr)).astype(vt_ref.dtype)
    def split(t):
        hi = t.astype(BF16)
        return hi, (t - hi.astype(F32)).astype(BF16)

    def dot_nt(a, b):
        return lax.dot_general(a, b, NT_DIMS, preferred_element_type=F32)

    bb_hi, bb_lo = split(jnp.concatenate([bbr, bbi], axis=1))
    kts = []
    for tau in range(S5_T):
        pr, pi = pows[tau]
        ct_hi, ct_lo = split(jnp.concatenate([cr * pr - ci * pi, -(cr * pi + ci * pr)], axis=1))
        kts.append(dot_nt(bb_hi, ct_hi) + dot_nt(bb_hi, ct_lo) + dot_nt(bb_lo, ct_hi))
    zero = jnp.zeros((LANES, LANES), F32)
    for r in range(S5_T):
        for e in range(2):
            tau = S5_T - 2 - r + e
            kt = kts[tau] if tau >= 0 else zero
            g_ref[r * LANES:(r + 1) * LANES, e * LANES:(e + 1) * LANES] = kt.astype(g_ref.dtype)
    for r in range(2 * SUBLANES):
        pr, pi = lam_pow(S5_T * r) if r <= SUBLANES else (jnp.zeros_like(lre), jnp.zeros_like(lre))
        lp_ref[r:r + 1, 0:S5_SB] = pr
        lp_ref[r:r + 1, S5_SB:] = pi


def _s5_prep(lam_re, lam_im, log_dt, b_re, b_im, c_re, c_im):
    nl, ng = lam_re.shape[:2]
    nblk = nl * ng // S5_GPB
    eye = jnp.eye(S5_GPB, dtype=F32)

    def rows(t):
        return t.astype(F32).reshape(nblk, 1, S5_SB)

    def embed(t):
        t = t.astype(F32).reshape(nblk, S5_GPB, S5_GROUP, S5_STATE)
        t = t[:, :, :, None, :] * eye[None, :, None, :, None]
        return t.reshape(nblk, LANES, S5_SB)

    dt_rows = jnp.broadcast_to(log_dt.astype(F32)[:, :, None], (nl, ng, S5_STATE))
    ins = [rows(lam_re), rows(lam_im), rows(dt_rows),
           embed(jnp.swapaxes(b_re, 2, 3)), embed(jnp.swapaxes(b_im, 2, 3)),
           embed(c_re), embed(c_im)]
    row_spec = pl.BlockSpec((None, 1, S5_SB), lambda k: (k, 0, 0))
    mat_spec = pl.BlockSpec((None, LANES, S5_SB), lambda k: (k, 0, 0))
    kt = S5_T * LANES
    out_shape = [jax.ShapeDtypeStruct((nblk, kt, 2 * LANES), BF16),
                 jax.ShapeDtypeStruct((nblk, kt, 2 * S5_SB), BF16),
                 jax.ShapeDtypeStruct((nblk, kt, 2 * S5_SB), BF16),
                 jax.ShapeDtypeStruct((nblk, 2 * SUBLANES, 2 * S5_SB), F32)]
    out_specs = [pl.BlockSpec((None,) + s.shape[1:], lambda k: (k, 0, 0)) for s in out_shape]
    return pl.pallas_call(
        _s5_prep_kernel, grid=(nblk,),
        in_specs=[row_spec] * 3 + [mat_spec] * 4, out_specs=out_specs, out_shape=out_shape,
        compiler_params=_cparams(("parallel",), 48), name="s5_prep",
    )(*ins)


def _s5_kernel(u_ref, g_ref, wz_ref, vt_ref, lp_ref, d_ref, o_ref, u16_ref, z_ref, s_ref, y_ref):
    nc = u16_ref.shape[0]
    kq = S5_T * LANES // 4
    z = None
    for q in range(4):
        for j in range(q * S5_T // 4, (q + 1) * S5_T // 4):
            u16_ref[:, j * LANES:(j + 1) * LANES] = u_ref[pl.ds(j, nc, stride=S5_T), :].astype(BF16)
        zq = jnp.dot(u16_ref[:, q * kq:(q + 1) * kq], wz_ref[q * kq:(q + 1) * kq, :],
                     preferred_element_type=F32)
        z = zq if z is None else z + zq
    z_ref[...] = z

    row = lax.broadcasted_iota(jnp.int32, (SUBLANES, S5_SB), 0)

    def shift(x, sh):
        return jnp.where(row >= sh, pltpu.roll(x, sh, axis=0), 0.0)

    def lp(r0, r1):
        return lp_ref[r0:r1, 0:S5_SB], lp_ref[r0:r1, S5_SB:]

    def scan_rows(r0, carry):
        cr, ci = carry
        xr = z_ref[r0:r0 + SUBLANES, 0:S5_SB]
        xi = z_ref[r0:r0 + SUBLANES, S5_SB:]
        for sh in (1, 2, 4):
            pr, pi = lp(sh, sh + 1)
            sr, si = shift(xr, sh), shift(xi, sh)
            xr, xi = xr + pr * sr - pi * si, xi + pr * si + pi * sr
        pr, pi = lp(0, SUBLANES)
        s_ref[r0:r0 + SUBLANES, 0:S5_SB] = (pr * cr - pi * ci + shift(xr, 1)).astype(s_ref.dtype)
        s_ref[r0:r0 + SUBLANES, S5_SB:] = (pr * ci + pi * cr + shift(xi, 1)).astype(s_ref.dtype)
        pr, pi = lp(SUBLANES, SUBLANES + 1)
        last = SUBLANES - 1
        return (pr * cr - pi * ci + xr[last:last + 1], pr * ci + pi * cr + xi[last:last + 1])

    pairs = S5_T // 2
    for ip in range(pairs):
        kk = (2 * ip + 2) * LANES
        y_ref[:, 2 * ip * LANES:(2 * ip + 2) * LANES] = jnp.dot(
            u16_ref[:, 0:kk], g_ref[(S5_T - 2 - 2 * ip) * LANES:, :], preferred_element_type=F32)

    carry = (jnp.zeros((1, S5_SB), F32), jnp.zeros((1, S5_SB), F32))
    for r0 in range(0, nc, SUBLANES):
        carry = scan_rows(r0, carry)

    sb = s_ref[...].astype(BF16)
    d = d_ref[...]
    for ip in range(pairs):
        acc = y_ref[:, 2 * ip * LANES:(2 * ip + 2) * LANES] + lax.dot_general(
            sb, vt_ref[2 * ip * LANES:(2 * ip + 2) * LANES, :], NT_DIMS, preferred_element_type=F32)
        for e in range(2):
            i = 2 * ip + e
            ui = u_ref[pl.ds(i, nc, stride=S5_T), :]
            o_ref[pl.ds(i, nc, stride=S5_T), :] = acc[:, e * LANES:(e + 1) * LANES] + d * ui


def _s5_apply(a_in, gm, wz, vt, lp, d, layer):
    bsz, seq, width = a_in.shape
    nblk = width // LANES
    nc = seq // S5_T
    kt = S5_T * LANES
    act_spec = pl.BlockSpec((None, seq, LANES), lambda k, b: (b, 0, k))

    def wspec(shape):
        return pl.BlockSpec((None,) + shape, lambda k, b: (layer * nblk + k, 0, 0))

    return pl.pallas_call(
        _s5_kernel, grid=(nblk, bsz),
        in_specs=[act_spec, wspec((kt, 2 * LANES)), wspec((kt, 2 * S5_SB)), wspec((kt, 2 * S5_SB)),
                  wspec((2 * SUBLANES, 2 * S5_SB)), pl.BlockSpec((None, 1, LANES), lambda k, b: (layer, 0, k))],
        out_specs=act_spec,
        out_shape=jax.ShapeDtypeStruct((bsz, seq, width), F32),
        scratch_shapes=[pltpu.VMEM((nc, kt), BF16), pltpu.VMEM((nc, 2 * S5_SB), F32),
                        pltpu.VMEM((nc, 2 * S5_SB), F32), pltpu.VMEM((nc, kt), F32)],
        compiler_params=_cparams(("parallel", "parallel"), 56), name="s5_apply",
    )(a_in, gm, wz, vt, lp, d.astype(F32).reshape(d.shape[0], 1, width))


def _even_tail_kernel(x_ref, ya_ref, sa_ref, hin_ref, sb_ref,
                      wglu_ref, bglu_ref, cw_ref, cb_ref, lng_ref, lnb_ref,
                      wpw_ref, bpw_ref, wout_ref, gpost_ref, o_ref, h_ref, hs_ref, c_ref, *, rows):
    tm, w = c_ref.shape
    nsh = hs_ref.shape[1]

    @pl.when(pl.program_id(1) == 0)
    def _():
        h_ref[0:CONV_HALO, :] = jnp.zeros((CONV_HALO, w), F32)

    h_ref[CONV_HALO:, :] = hin_ref[...].astype(F32)
    for s in range(1, SUBLANES):
        hs_ref[s - 1] = h_ref[s:s + nsh, :]

    off = CONV_HALO - (CONV_K - 1)

    def conv_rows(ri, _):
        r0 = pl.multiple_of(ri * rows, rows)
        for c0 in range(0, w, LANES):
            acc = jnp.broadcast_to(cb_ref[:, c0:c0 + LANES], (rows, LANES))
            for s in range(SUBLANES):
                taps = [o for o in range(off, off + CONV_K) if o % SUBLANES == s]
                src = h_ref if s == 0 else hs_ref.at[s - 1]
                lo, hi = taps[0] - s, taps[-1] - s + rows
                win = src[pl.ds(pl.multiple_of(r0 + lo, SUBLANES), hi - lo), c0:c0 + LANES]
                for o in taps:
                    k = o - off
                    acc = acc + cw_ref[k:k + 1, c0:c0 + LANES] * win[o - s - lo:o - s - lo + rows]
            c_ref[pl.ds(r0, rows), c0:c0 + LANES] = acc
        return 0

    lax.fori_loop(0, tm // rows, conv_rows, 0)
    h_ref[0:CONV_HALO, :] = h_ref[tm:tm + CONV_HALO, :]

    hc = c_ref[...]
    mu = jnp.mean(hc, axis=-1, keepdims=True)
    xc = hc - mu
    hn = xc * lax.rsqrt(jnp.mean(xc * xc, axis=-1, keepdims=True) + EPS) * lng_ref[...] + lnb_ref[...]
    yb = jnp.dot(_silu(hn).astype(BF16), wpw_ref[...], preferred_element_type=F32) + bpw_ref[...]
    yb = yb * sb_ref[...].astype(F32)

    ya = jax.nn.gelu(ya_ref[...])
    gate = jnp.dot(ya.astype(BF16), wglu_ref[...], preferred_element_type=F32) + bglu_ref[...]
    ya = ya * _sigmoid(gate) * sa_ref[...].astype(F32)

    y = jnp.dot(ya.astype(BF16), wout_ref[0:w, :], preferred_element_type=F32)
    y = y + jnp.dot(yb.astype(BF16), wout_ref[w:, :], preferred_element_type=F32)
    o_ref[...] = x_ref[...] + y * _rms_scale(y) * gpost_ref[...]


def _even_tail(x, ya, p, i, wglu_b, bglu, cw, cb, lng, lnb, wpw_b, bpw, wout_b, norm_post, tm=512, rows=32):
    bsz, seq, d = x.shape
    w = ya.shape[-1]
    j = i // 2

    def act(col):
        return pl.BlockSpec((None, tm, w), lambda b, t: (b, t, col))

    vec = lambda v: v.astype(F32).reshape(v.shape[0], 1, v.shape[1])
    cw_pad = jnp.pad(cw.astype(F32), ((0, 0), (0, CONV_HALO - CONV_K), (0, 0)))
    consts = [wglu_b, vec(bglu), cw_pad, vec(cb), vec(lng), vec(lnb), wpw_b, vec(bpw), wout_b]
    const_specs = [_layer_spec(c.shape[1:], j) for c in consts] + [_layer_spec((1, d), i)]
    consts.append(vec(norm_post))
    return pl.pallas_call(
        functools.partial(_even_tail_kernel, rows=rows), grid=(bsz, seq // tm),
        in_specs=[act(0), act(0), act(0), act(1), act(2)] + const_specs,
        out_specs=act(0),
        out_shape=jax.ShapeDtypeStruct((bsz, seq, d), F32),
        scratch_shapes=[pltpu.VMEM((CONV_HALO + tm, w), F32),
                        pltpu.VMEM((SUBLANES - 1, CONV_HALO + tm - SUBLANES, w), F32),
                        pltpu.VMEM((tm, w), F32)],
        compiler_params=_cparams(("parallel", "arbitrary"), 56), name="even_tail",
    )(x, ya, p, p, p, *consts)


_GLA_C = GLA_CHUNK
_GLA_TB = 4 * _GLA_C
_R_QD, _R_KD, _R_KU, _R_QX, _R_KX = (i * _GLA_TB for i in range(5))
_R_QHI, _R_KLO, _R_KB0, _R_KB1 = (5 * _GLA_TB + i * 2 * _GLA_C for i in range(4))
_GLA_OPS_ROWS = 7 * _GLA_TB


def _gla_prep(q_b, k_b, lr, wg, bg, ops_ref, dec_ref):
    tb, hk = q_b.shape
    c = _GLA_C
    assert tb == _GLA_TB
    tpc = c // SUBLANES

    gp = jnp.dot(lr, wg, preferred_element_type=F32) + bg
    g = (jnp.minimum(gp, 0.0) - jnp.log(1.0 + jnp.exp(-jnp.abs(gp)))) * (1.0 / GLA_TAU)

    row8 = lax.broadcasted_iota(jnp.int32, (SUBLANES, hk), 0)
    tiles = []
    for i in range(tb // SUBLANES):
        x = g[i * SUBLANES:(i + 1) * SUBLANES]
        for sh in (1, 2, 4):
            x = x + jnp.where(row8 >= sh, pltpu.roll(x, sh, axis=0), 0.0)
        if i % tpc:
            x = x + tiles[-1][SUBLANES - 1:SUBLANES, :]
        tiles.append(x)
    bc = jnp.concatenate(tiles, axis=0)
    bl = [tiles[(a + 1) * tpc - 1][SUBLANES - 1:SUBLANES, :] for a in range(tb // c)]
    bl_rows = jnp.concatenate([jnp.broadcast_to(b, (c, hk)) for b in bl], axis=0)

    q = q_b.astype(F32) * (GLA_HK ** -0.5)
    k = k_b.astype(F32)
    qd = q * jnp.exp(bc)
    kd = k * jnp.exp(-bc)
    ku = k * jnp.exp(bl_rows - bc)
    qd_b, kd_b, ku_b = qd.astype(BF16), kd.astype(BF16), ku.astype(BF16)
    ops_ref[_R_QD:_R_QD + tb] = qd_b
    ops_ref[_R_KD:_R_KD + tb] = kd_b
    ops_ref[_R_KU:_R_KU + tb] = ku_b

    def rows(t, a):
        return t[a * c:(a + 1) * c]

    def put(r0, a, val):
        ops_ref[r0 + a * c:r0 + (a + 1) * c] = val

    e0, e1, e2, e3 = (jnp.exp(b) for b in bl)
    e01 = jnp.exp(bl[0] + bl[1])
    e012 = jnp.exp(bl[0] + bl[1] + bl[2])
    e23 = jnp.exp(bl[2] + bl[3])
    e123 = jnp.exp(bl[1] + bl[2] + bl[3])
    dec_ref[...] = jnp.broadcast_to(jnp.exp(bl[0] + bl[1] + bl[2] + bl[3]), dec_ref.shape)
    put(_R_QX, 0, rows(qd_b, 0))
    put(_R_QX, 1, (rows(qd, 1) * e0).astype(BF16))
    put(_R_QX, 2, (rows(qd, 2) * e01).astype(BF16))
    put(_R_QX, 3, (rows(qd, 3) * e012).astype(BF16))
    put(_R_KX, 0, (rows(ku, 0) * e123).astype(BF16))
    put(_R_KX, 1, (rows(ku, 1) * e23).astype(BF16))
    put(_R_KX, 2, (rows(ku, 2) * e3).astype(BF16))
    put(_R_KX, 3, rows(ku_b, 3))
    put(_R_QHI, 0, rows(qd_b, 2))
    put(_R_QHI, 1, (rows(qd, 3) * e2).astype(BF16))
    put(_R_KLO, 0, (rows(ku, 0) * e1).astype(BF16))
    put(_R_KLO, 1, rows(ku_b, 1))
    put(_R_KB0, 0, rows(ku_b, 0))
    put(_R_KB0, 1, rows(kd_b, 1))
    put(_R_KB1, 0, rows(ku_b, 2))
    put(_R_KB1, 1, rows(kd_b, 3))


def _gla_mix(ops_ref, dec_ref, v, r_b, ng, st_ref):
    c = _GLA_C
    half = 2 * c

    def op(r0, n, a=0):
        return ops_ref[r0 + a * c:r0 + (a + n) * c]

    cross = lax.dot_general(op(_R_QHI, 2), op(_R_KLO, 2), NT_DIMS, preferred_element_type=F32).astype(BF16)
    ri = lax.broadcasted_iota(jnp.int32, (half, half), 0)
    ci = lax.broadcasted_iota(jnp.int32, (half, half), 1)
    causal = ri >= ci

    def half_attn(h):
        top = lax.dot_general(op(_R_QD, 1, 2 * h), op(_R_KD, 2, 2 * h), NT_DIMS, preferred_element_type=F32)
        bot = lax.dot_general(op(_R_QD, 1, 2 * h + 1), op(_R_KB1 if h else _R_KB0, 2), NT_DIMS,
                              preferred_element_type=F32)
        return jnp.where(causal, jnp.concatenate([top, bot], axis=0), 0.0).astype(BF16)

    st = st_ref[...]
    o_top = jnp.dot(half_attn(0), v[0:half], preferred_element_type=F32)
    o_bot = jnp.dot(jnp.concatenate([cross, half_attn(1)], axis=1), v, preferred_element_type=F32)
    o = jnp.concatenate([o_top, o_bot], axis=0)
    o = o + lax.dot_general(op(_R_QX, 4), st.astype(BF16), NT_DIMS, preferred_element_type=F32)
    st_ref[...] = st * dec_ref[0:1, :] + lax.dot_general(v, op(_R_KX, 4), TN_DIMS, preferred_element_type=F32)
    on = o * _rms_scale(o) * ng
    return on * _silu(r_b.astype(F32))


def _gla_kernel(x_ref, q_ref, k_ref, v_ref, r_ref, lr_ref, wg_ref, bg_ref, ng_ref, wout_ref, gpost_ref,
                o_ref, st_ref, ops_ref, dec_ref):
    @pl.when(pl.program_id(1) == 0)
    def _():
        st_ref[...] = jnp.zeros_like(st_ref)

    hk, hv = GLA_HK, GLA_HV
    nheads = st_ref.shape[0]
    lr, ng = lr_ref[...], ng_ref[...]

    def prep(j):
        ks = slice(j * hk, (j + 1) * hk)
        _gla_prep(q_ref[:, ks], k_ref[:, ks], lr, wg_ref[:, ks], bg_ref[:, ks], ops_ref.at[j], dec_ref.at[j])

    def mix(j):
        vs = slice(j * hv, (j + 1) * hv)
        out = _gla_mix(ops_ref.at[j], dec_ref.at[j], v_ref[:, vs], r_ref[:, vs], ng, st_ref.at[j])
        return jnp.dot(out.astype(BF16), wout_ref[vs, :], preferred_element_type=F32)

    prep(0)
    y = None
    for j in range(nheads):
        if j + 1 < nheads:
            prep(j + 1)
        part = mix(j)
        y = part if y is None else y + part
    o_ref[...] = x_ref[...] + y * _rms_scale(y) * gpost_ref[...]


def _gla_out(x, p, p_lr, wg, bg, ng, w_out_b, wi, norm_post, gi, tb=_GLA_TB):
    bsz, seq, d = x.shape
    nh = GLA_HEADS
    dk, dv = nh * GLA_HK, nh * GLA_HV

    def act(width, col):
        return pl.BlockSpec((None, tb, width), lambda b, t: (b, t, col))

    in_specs = [act(d, 0), act(dk, 0), act(dk, 1), act(dv, 2 * dk // dv), act(dv, (2 * dk + dv) // dv),
                act(LANES, 0), _const_spec((LANES, dk)), _const_spec((1, dk)), _const_spec((1, GLA_HV)),
                _layer_spec(w_out_b.shape[1:], wi), _layer_spec((1, d), gi)]
    wg_pad = jnp.pad(wg.astype(BF16), ((0, LANES - GLA_LOWRANK), (0, 0)))
    return pl.pallas_call(
        _gla_kernel, grid=(bsz, seq // tb), in_specs=in_specs,
        out_specs=act(d, 0),
        out_shape=jax.ShapeDtypeStruct((bsz, seq, d), F32),
        scratch_shapes=[pltpu.VMEM((nh, GLA_HV, GLA_HK), F32),
                        pltpu.VMEM((nh, _GLA_OPS_ROWS, GLA_HK), BF16),
                        pltpu.VMEM((nh, SUBLANES, GLA_HK), F32)],
        compiler_params=_cparams(("parallel", "arbitrary"), 48), name="gla_out",
    )(x, p, p, p, p, p_lr, wg_pad, bg.reshape(1, dk).astype(F32), ng.reshape(1, GLA_HV).astype(F32),
      w_out_b, norm_post.reshape(norm_post.shape[0], 1, d))


def _even_layer(x, i, norm_pre, norm_post, w_in_b, s5_ops, s5_d, w_glu_b, b_glu, cw, cb, lng, lnb, w_pw_b, b_pw,
                w_out_b):
    bsz, seq, d = x.shape
    j = i // 2
    wa = s5_d.shape[1]
    a_in, gates = _even_in(x.reshape(bsz * seq, d), norm_pre, i, w_in_b, j, wa)
    ya = _s5_apply(a_in.reshape(bsz, seq, wa), *s5_ops, s5_d, j)
    return _even_tail(x, ya, gates.reshape(bsz, seq, -1), i, w_glu_b, b_glu, cw, cb, lng, lnb,
                      w_pw_b, b_pw, w_out_b, norm_post)


def _odd_layer(x, i, norm_pre, norm_post, w_main_b, w_lr_b, wg, bg, ng, w_out_b):
    bsz, seq, d = x.shape
    j = i // 2
    p, p_lr = _norm_proj(x.reshape(bsz * seq, d), norm_pre, i, [w_main_b, w_lr_b], j,
                         [[w_main_b.shape[2]], [w_lr_b.shape[2]]], [BF16, BF16])
    return _gla_out(x, p.reshape(bsz, seq, -1), p_lr.reshape(bsz, seq, -1), wg, bg, ng, w_out_b, j, norm_post, i)


def kernel(x, norm_pre, norm_post, ev_w_in, s5_lambda_re, s5_lambda_im, s5_log_dt, s5_b_re, s5_b_im, s5_c_re, s5_c_im, s5_d, s5_w_glu, s5_b_glu, conv_w, conv_b, conv_ln_g, conv_ln_b, conv_w_pw, conv_b_pw, ev_w_out, od_w_in, gla_w_gate_up, gla_b_gate, gla_norm_g, od_w_out):
    depth = norm_pre.shape[0]
    norm_pre, norm_post = norm_pre.astype(F32), norm_post.astype(F32)
    ev_w_in_b, s5_w_glu_b, conv_w_pw_b = ev_w_in.astype(BF16), s5_w_glu.astype(BF16), conv_w_pw.astype(BF16)
    ev_w_out_b, od_w_out_b = ev_w_out.astype(BF16), od_w_out.astype(BF16)
    n_main = od_w_in.shape[2] - GLA_LOWRANK
    od_main_b = od_w_in[:, :, :n_main].astype(BF16)
    od_lr_b = jnp.pad(od_w_in[:, :, n_main:], ((0, 0), (0, 0), (0, LANES - GLA_LOWRANK))).astype(BF16)
    s5_ops = _s5_prep(s5_lambda_re, s5_lambda_im, s5_log_dt, s5_b_re, s5_b_im, s5_c_re, s5_c_im)
    for i in range(depth):
        j = i // 2
        if i % 2 == 0:
            x = _even_layer(x, i, norm_pre, norm_post, ev_w_in_b, s5_ops, s5_d, s5_w_glu_b,
                            s5_b_glu, conv_w, conv_b, conv_ln_g, conv_ln_b, conv_w_pw_b, conv_b_pw, ev_w_out_b)
        else:
            x = _odd_layer(x, i, norm_pre, norm_post, od_main_b, od_lr_b, gla_w_gate_up[j], gla_b_gate[j],
                           gla_norm_g[j], od_w_out_b)
    return x
```

```python
import functools

import jax
import jax.numpy as jnp
from jax import lax
from jax.experimental import pallas as pl
from jax.experimental.pallas import tpu as pltpu

F32 = jnp.float32
BF16 = jnp.bfloat16

EPS = 1e-6
LANES = 128
SUBLANES = 8

S5_GROUP = 16
S5_STATE = 64
S5_T = 16
S5_GPB = LANES // S5_GROUP
S5_SB = S5_GPB * S5_STATE
CONV_K = 31
CONV_HALO = 32
GLA_HEADS = 4
GLA_HK = 256
GLA_HV = 512
GLA_LOWRANK = 16
GLA_TAU = 16.0
GLA_CHUNK = 64

NT_DIMS = (((1,), (1,)), ((), ()))
TN_DIMS = (((0,), (0,)), ((), ()))


def _cparams(sem, vmem_mb):
    return pltpu.CompilerParams(dimension_semantics=sem, vmem_limit_bytes=vmem_mb * 1024 * 1024)


def _const_spec(shape):
    nd = len(shape)
    return pl.BlockSpec(shape, lambda *_: (0,) * nd, pipeline_mode=pl.Buffered(1))


def _rms_scale(x):
    return lax.rsqrt(jnp.mean(x * x, axis=-1, keepdims=True) + EPS)


def _sigmoid(x):
    return 0.5 * jnp.tanh(0.5 * x) + 0.5


def _silu(x):
    return x * _sigmoid(x)


def _layer_spec(shape, layer):
    nd = len(shape)
    return pl.BlockSpec((None,) + tuple(shape), lambda *_: (layer,) + (0,) * nd, pipeline_mode=pl.Buffered(1))


def _norm_proj_kernel(x_ref, g_ref, *refs, n_w, splits, n_chunk):
    w_refs, o_refs = refs[:n_w], refs[n_w:]
    x = x_ref[...]
    u = (x * _rms_scale(x) * g_ref[...]).astype(BF16)
    outs = iter(o_refs)
    for w_ref, widths in zip(w_refs, splits):
        base = 0
        for width in widths:
            o_ref = next(outs)
            for n0 in range(0, width, n_chunk):
                n1 = min(n0 + n_chunk, width)
                o_ref[:, n0:n1] = jnp.dot(u, w_ref[:, base + n0:base + n1],
                                          preferred_element_type=F32).astype(o_ref.dtype)
            base += width


def _norm_proj(x2, g_stack, gi, w_stacks, wi, splits, out_dtypes, tm=512, n_chunk=512):
    m, d = x2.shape
    widths = [wd for ws in splits for wd in ws]
    in_specs = [pl.BlockSpec((tm, d), lambda i: (i, 0)), _layer_spec((1, d), gi)]
    in_specs += [_layer_spec((d, sum(ws)), wi) for ws in splits]
    out_specs = [pl.BlockSpec((tm, wd), lambda i: (i, 0)) for wd in widths]
    out_shape = [jax.ShapeDtypeStruct((m, wd), dt) for wd, dt in zip(widths, out_dtypes)]
    return pl.pallas_call(
        functools.partial(_norm_proj_kernel, n_w=len(w_stacks), splits=splits, n_chunk=n_chunk),
        grid=(m // tm,), in_specs=in_specs, out_specs=out_specs, out_shape=out_shape,
        compiler_params=_cparams(("parallel",), 56), name="norm_proj",
    )(x2, g_stack.reshape(g_stack.shape[0], 1, d), *w_stacks)


def _even_in_kernel(x_ref, g_ref, w_ref, a_ref, gates_ref, *, rows, n_chunk):
    tm, wa = a_ref.shape
    for r0 in range(0, tm, rows):
        rs = slice(r0, r0 + rows)
        x = x_ref[rs, :]
        u = (x * _rms_scale(x) * g_ref[...]).astype(BF16)

        def proj(part, n0):
            c0 = part * wa + n0
            return jnp.dot(u, w_ref[:, c0:c0 + n_chunk], preferred_element_type=F32)

        for n0 in range(0, wa, n_chunk):
            cols = slice(n0, n0 + n_chunk)
            a_ref[rs, cols] = proj(0, n0)
            gates_ref[rs, cols] = _silu(proj(1, n0)).astype(gates_ref.dtype)
            gates_ref[rs, wa + n0:wa + n0 + n_chunk] = (
                proj(2, n0) * _sigmoid(proj(3, n0))).astype(gates_ref.dtype)
            gates_ref[rs, 2 * wa + n0:2 * wa + n0 + n_chunk] = _silu(proj(4, n0)).astype(gates_ref.dtype)


def _even_in(x2, g_stack, gi, w_stack, wi, wa, tm=1024, rows=512, n_chunk=512):
    m, d = x2.shape
    assert w_stack.shape[2] == 5 * wa and wa % n_chunk == 0
    return pl.pallas_call(
        functools.partial(_even_in_kernel, rows=rows, n_chunk=n_chunk), grid=(m // tm,),
        in_specs=[pl.BlockSpec((tm, d), lambda i: (i, 0)), _layer_spec((1, d), gi),
                  _layer_spec(w_stack.shape[1:], wi)],
        out_specs=[pl.BlockSpec((tm, wa), lambda i: (i, 0)), pl.BlockSpec((tm, 3 * wa), lambda i: (i, 0))],
        out_shape=[jax.ShapeDtypeStruct((m, wa), F32), jax.ShapeDtypeStruct((m, 3 * wa), BF16)],
        compiler_params=_cparams(("parallel",), 56), name="even_in",
    )(x2, g_stack.reshape(g_stack.shape[0], 1, d), w_stack)


def _s5_prep_kernel(lre_ref, lim_ref, dt_ref, btr_ref, bti_ref, cr_ref, ci_ref,
                    g_ref, wz_ref, vt_ref, lp_ref, own_ref):
    lre, lim, dt = lre_ref[...], lim_ref[...], jnp.exp(dt_ref[...])

    def lam_pow(m):
        mag = jnp.exp(lre * dt * float(m))
        ang = lim * dt * float(m)
        return mag * jnp.cos(ang), mag * jnp.sin(ang)

    def per_channel(t):
        return jnp.concatenate([jnp.broadcast_to(t[g:g + 1], (S5_GROUP, S5_STATE)) for g in range(S5_GPB)],
                               axis=0)

    def lane_tile(t):
        t2 = jnp.concatenate([t, t], axis=1)
        return jnp.concatenate([t2] * (S5_SB // LANES), axis=1)

    pows_g = [lam_pow(m) for m in range(S5_T + 1)]
    pows = [(per_channel(pr), per_channel(pi)) for pr, pi in pows_g]
    nr, ni = pows_g[1][0] - 1.0, pows_g[1][1]
    den = lre * lre + lim * lim
    cfr = per_channel((nr * lre + ni * lim) / den)
    cfi = per_channel((ni * lre - nr * lim) / den)
    btr, bti = btr_ref[...], bti_ref[...]
    bbr = cfr * btr - cfi * bti
    bbi = cfr * bti + cfi * btr
    cr, ci = cr_ref[...], ci_ref[...]

    row_g = lax.broadcasted_iota(jnp.int32, (LANES, S5_SB), 0) // S5_GROUP
    lane_g = lax.broadcasted_iota(jnp.int32, (LANES, S5_SB), 1) // S5_STATE
    own_ref[...] = jnp.where(row_g == lane_g, 1.0, 0.0)

    def place(re, im):
        def own(t):
            return jnp.where(own_ref[...] != 0.0, lane_tile(t), 0.0)
        return jnp.concatenate([own(re), own(im)], axis=1).astype(BF16)

    def hi_lo(re, im):
        re_hi, im_hi = re.astype(BF16).astype(F32), im.astype(BF16).astype(F32)
        return place(re_hi, im_hi), place(re - re_hi, im - im_hi)

    def dot_nt(a, b):
        return lax.dot_general(a, b, NT_DIMS, preferred_element_type=F32)

    for j in range(S5_T - 1):
        pr, pi = pows[S5_T - 1 - j]
        wz_ref[j * LANES:(j + 1) * LANES, :] = place(pr * bbr - pi * bbi, pr * bbi + pi * bbr)
    bb_hi, bb_lo = hi_lo(bbr, bbi)
    wz_ref[(S5_T - 1) * LANES:, :] = bb_hi
    kts = []
    for tau in range(S5_T + 1):
        pr, pi = pows[tau]
        ct_hi, ct_lo = hi_lo(cr * pr - ci * pi, -(cr * pi + ci * pr))
        if tau >= 1:
            vt_ref[(tau - 1) * LANES:tau * LANES, :] = ct_hi
        if tau < S5_T:
            kts.append(dot_nt(bb_hi, ct_hi) + dot_nt(bb_hi, ct_lo) + dot_nt(bb_lo, ct_hi))
    zero = jnp.zeros((LANES, LANES), F32)
    for r in range(S5_T):
        for e in range(2):
            tau = S5_T - 2 - r + e
            kt = kts[tau] if tau >= 0 else zero
            g_ref[r * LANES:(r + 1) * LANES, e * LANES:(e + 1) * LANES] = kt.astype(g_ref.dtype)
    own_row = (lax.broadcasted_iota(jnp.int32, (S5_GPB, S5_SB), 0)
               == lax.broadcasted_iota(jnp.int32, (S5_GPB, S5_SB), 1) // S5_STATE)

    def as_row(t):
        return jnp.sum(jnp.where(own_row, lane_tile(t), 0.0), axis=0, keepdims=True)

    lp_ref[...] = jnp.zeros_like(lp_ref)
    for r in range(SUBLANES + 1):
        pr, pi = lam_pow(S5_T * r)
        lp_ref[r:r + 1, 0:S5_SB] = as_row(pr)
        lp_ref[r:r + 1, S5_SB:] = as_row(pi)


def _s5_prep(lam_re, lam_im, log_dt, b_re, b_im, c_re, c_im):
    nl, ng = lam_re.shape[:2]
    nblk = nl * ng // S5_GPB

    def groups(t):
        return t.astype(F32).reshape(nblk, S5_GPB, S5_STATE)

    def mats(t):
        return t.astype(F32).reshape(nblk, LANES, S5_STATE)

    dt_g = jnp.broadcast_to(log_dt.astype(F32)[:, :, None], (nl, ng, S5_STATE))
    ins = [groups(lam_re), groups(lam_im), groups(dt_g),
           mats(jnp.swapaxes(b_re, 2, 3)), mats(jnp.swapaxes(b_im, 2, 3)), mats(c_re), mats(c_im)]
    grp_spec = pl.BlockSpec((None, S5_GPB, S5_STATE), lambda k: (k, 0, 0))
    mat_spec = pl.BlockSpec((None, LANES, S5_STATE), lambda k: (k, 0, 0))
    kt = S5_T * LANES
    out_shape = [jax.ShapeDtypeStruct((nblk, kt, 2 * LANES), BF16),
                 jax.ShapeDtypeStruct((nblk, kt, 2 * S5_SB), BF16),
                 jax.ShapeDtypeStruct((nblk, kt, 2 * S5_SB), BF16),
                 jax.ShapeDtypeStruct((nblk, 2 * SUBLANES, 2 * S5_SB), F32)]
    out_specs = [pl.BlockSpec((None,) + s.shape[1:], lambda k: (k, 0, 0)) for s in out_shape]
    return pl.pallas_call(
        _s5_prep_kernel, grid=(nblk,),
        in_specs=[grp_spec] * 3 + [mat_spec] * 4, out_specs=out_specs, out_shape=out_shape,
        scratch_shapes=[pltpu.VMEM((LANES, S5_SB), F32)],
        compiler_params=_cparams(("parallel",), 48), name="s5_prep",
    )(*ins)


def _s5_kernel(u_ref, g_ref, wz_ref, vt_ref, lp_ref, d_ref, o_ref, u16_ref, z_ref, s_ref, y_ref):
    nc = u16_ref.shape[0]
    kq = S5_T * LANES // 4
    z = None
    for q in range(4):
        for j in range(q * S5_T // 4, (q + 1) * S5_T // 4):
            u16_ref[:, j * LANES:(j + 1) * LANES] = u_ref[pl.ds(j, nc, stride=S5_T), :].astype(BF16)
        zq = jnp.dot(u16_ref[:, q * kq:(q + 1) * kq], wz_ref[q * kq:(q + 1) * kq, :],
                     preferred_element_type=F32)
        z = zq if z is None else z + zq
    z_ref[...] = z

    row = lax.broadcasted_iota(jnp.int32, (SUBLANES, S5_SB), 0)

    def shift(x, sh):
        return jnp.where(row >= sh, pltpu.roll(x, sh, axis=0), 0.0)

    def lp(r0, r1):
        return lp_ref[r0:r1, 0:S5_SB], lp_ref[r0:r1, S5_SB:]

    def scan_rows(r0, carry):
        cr, ci = carry
        xr = z_ref[r0:r0 + SUBLANES, 0:S5_SB]
        xi = z_ref[r0:r0 + SUBLANES, S5_SB:]
        for sh in (1, 2, 4):
            pr, pi = lp(sh, sh + 1)
            sr, si = shift(xr, sh), shift(xi, sh)
            xr, xi = xr + pr * sr - pi * si, xi + pr * si + pi * sr
        pr, pi = lp(0, SUBLANES)
        s_ref[r0:r0 + SUBLANES, 0:S5_SB] = (pr * cr - pi * ci + shift(xr, 1)).astype(s_ref.dtype)
        s_ref[r0:r0 + SUBLANES, S5_SB:] = (pr * ci + pi * cr + shift(xi, 1)).astype(s_ref.dtype)
        pr, pi = lp(SUBLANES, SUBLANES + 1)
        last = SUBLANES - 1
        return (pr * cr - pi * ci + xr[last:last + 1], pr * ci + pi * cr + xi[last:last + 1])

    pairs = S5_T // 2
    for ip in range(pairs):
        kk = (2 * ip + 2) * LANES
        y_ref[:, 2 * ip * LANES:(2 * ip + 2) * LANES] = jnp.dot(
            u16_ref[:, 0:kk], g_ref[(S5_T - 2 - 2 * ip) * LANES:, :], preferred_element_type=F32)

    carry = (jnp.zeros((1, S5_SB), F32), jnp.zeros((1, S5_SB), F32))
    for r0 in range(0, nc, SUBLANES):
        carry = scan_rows(r0, carry)

    sb = s_ref[...].astype(BF16)
    d = d_ref[...]
    for ip in range(pairs):
        acc = y_ref[:, 2 * ip * LANES:(2 * ip + 2) * LANES] + lax.dot_general(
            sb, vt_ref[2 * ip * LANES:(2 * ip + 2) * LANES, :], NT_DIMS, preferred_element_type=F32)
        for e in range(2):
            i = 2 * ip + e
            ui = u_ref[pl.ds(i, nc, stride=S5_T), :]
            o_ref[pl.ds(i, nc, stride=S5_T), :] = acc[:, e * LANES:(e + 1) * LANES] + d * ui


def _s5_apply(a_in, gm, wz, vt, lp, d, layer):
    bsz, seq, width = a_in.shape
    nblk = width // LANES
    nc = seq // S5_T
    kt = S5_T * LANES
    act_spec = pl.BlockSpec((None, seq, LANES), lambda k, b: (b, 0, k))

    def wspec(shape):
        return pl.BlockSpec((None,) + shape, lambda k, b: (layer * nblk + k, 0, 0))

    return pl.pallas_call(
        _s5_kernel, grid=(nblk, bsz),
        in_specs=[act_spec, wspec((kt, 2 * LANES)), wspec((kt, 2 * S5_SB)), wspec((kt, 2 * S5_SB)),
                  wspec((2 * SUBLANES, 2 * S5_SB)), pl.BlockSpec((None, 1, LANES), lambda k, b: (layer, 0, k))],
        out_specs=act_spec,
        out_shape=jax.ShapeDtypeStruct((bsz, seq, width), F32),
        scratch_shapes=[pltpu.VMEM((nc, kt), BF16), pltpu.VMEM((nc, 2 * S5_SB), F32),
                        pltpu.VMEM((nc, 2 * S5_SB), F32), pltpu.VMEM((nc, kt), F32)],
        compiler_params=_cparams(("parallel", "parallel"), 56), name="s5_apply",
    )(a_in, gm, wz, vt, lp, d.astype(F32).reshape(d.shape[0], 1, width))


def _even_tail_kernel(x_ref, ya_ref, sa_ref, hin_ref, sb_ref,
                      wglu_ref, bglu_ref, cw_ref, cb_ref, lng_ref, lnb_ref,
                      wpw_ref, bpw_ref, wout_ref, gpost_ref, o_ref, h_ref, hs_ref, c_ref, *, rows):
    tm, w = c_ref.shape
    nsh = hs_ref.shape[1]

    @pl.when(pl.program_id(1) == 0)
    def _():
        h_ref[0:CONV_HALO, :] = jnp.zeros((CONV_HALO, w), F32)

    h_ref[CONV_HALO:, :] = hin_ref[...].astype(F32)
    for s in range(1, SUBLANES):
        hs_ref[s - 1] = h_ref[s:s + nsh, :]

    off = CONV_HALO - (CONV_K - 1)

    def conv_rows(ri, _):
        r0 = pl.multiple_of(ri * rows, rows)
        for c0 in range(0, w, LANES):
            acc = jnp.broadcast_to(cb_ref[:, c0:c0 + LANES], (rows, LANES))
            for s in range(SUBLANES):
                taps = [o for o in range(off, off + CONV_K) if o % SUBLANES == s]
                src = h_ref if s == 0 else hs_ref.at[s - 1]
                lo, hi = taps[0] - s, taps[-1] - s + rows
                win = src[pl.ds(pl.multiple_of(r0 + lo, SUBLANES), hi - lo), c0:c0 + LANES]
                for o in taps:
                    k = o - off
                    acc = acc + cw_ref[k:k + 1, c0:c0 + LANES] * win[o - s - lo:o - s - lo + rows]
            c_ref[pl.ds(r0, rows), c0:c0 + LANES] = acc
        return 0

    lax.fori_loop(0, tm // rows, conv_rows, 0)
    h_ref[0:CONV_HALO, :] = h_ref[tm:tm + CONV_HALO, :]

    hc = c_ref[...]
    mu = jnp.mean(hc, axis=-1, keepdims=True)
    xc = hc - mu
    hn = xc * lax.rsqrt(jnp.mean(xc * xc, axis=-1, keepdims=True) + EPS) * lng_ref[...] + lnb_ref[...]
    yb = jnp.dot(_silu(hn).astype(BF16), wpw_ref[...], preferred_element_type=F32) + bpw_ref[...]
    yb = yb * sb_ref[...].astype(F32)

    ya = jax.nn.gelu(ya_ref[...])
    gate = jnp.dot(ya.astype(BF16), wglu_ref[...], preferred_element_type=F32) + bglu_ref[...]
    ya = ya * _sigmoid(gate) * sa_ref[...].astype(F32)

    y = jnp.dot(ya.astype(BF16), wout_ref[0:w, :], preferred_element_type=F32)
    y = y + jnp.dot(yb.astype(BF16), wout_ref[w:, :], preferred_element_type=F32)
    o_ref[...] = x_ref[...] + y * _rms_scale(y) * gpost_ref[...]


def _even_tail(x, ya, p, i, wglu_b, bglu, cw, cb, lng, lnb, wpw_b, bpw, wout_b, norm_post, tm=512, rows=32):
    bsz, seq, d = x.shape
    w = ya.shape[-1]
    j = i // 2

    def act(col):
        return pl.BlockSpec((None, tm, w), lambda b, t: (b, t, col))

    vec = lambda v: v.astype(F32).reshape(v.shape[0], 1, v.shape[1])
    cw_pad = jnp.pad(cw.astype(F32), ((0, 0), (0, CONV_HALO - CONV_K), (0, 0)))
    consts = [wglu_b, vec(bglu), cw_pad, vec(cb), vec(lng), vec(lnb), wpw_b, vec(bpw), wout_b]
    const_specs = [_layer_spec(c.shape[1:], j) for c in consts] + [_layer_spec((1, d), i)]
    consts.append(vec(norm_post))
    return pl.pallas_call(
        functools.partial(_even_tail_kernel, rows=rows), grid=(bsz, seq // tm),
        in_specs=[act(0), act(0), act(0), act(1), act(2)] + const_specs,
        out_specs=act(0),
        out_shape=jax.ShapeDtypeStruct((bsz, seq, d), F32),
        scratch_shapes=[pltpu.VMEM((CONV_HALO + tm, w), F32),
                        pltpu.VMEM((SUBLANES - 1, CONV_HALO + tm - SUBLANES, w), F32),
                        pltpu.VMEM((tm, w), F32)],
        compiler_params=_cparams(("parallel", "arbitrary"), 56), name="even_tail",
    )(x, ya, p, p, p, *consts)


_GLA_C = GLA_CHUNK
_GLA_TB = 4 * _GLA_C
_R_QD, _R_KD, _R_KU, _R_QX, _R_KX = (i * _GLA_TB for i in range(5))
_R_QHI, _R_KLO, _R_KB0, _R_KB1 = (5 * _GLA_TB + i * 2 * _GLA_C for i in range(4))
_GLA_OPS_ROWS = 7 * _GLA_TB


def _gla_prep(q_b, k_b, lr, wg, bg, ops_ref, dec_ref):
    tb, hk = q_b.shape
    c = _GLA_C
    assert tb == _GLA_TB
    tpc = c // SUBLANES

    gp = jnp.dot(lr, wg, preferred_element_type=F32) + bg
    g = (jnp.minimum(gp, 0.0) - jnp.log(1.0 + jnp.exp(-jnp.abs(gp)))) * (1.0 / GLA_TAU)

    row8 = lax.broadcasted_iota(jnp.int32, (SUBLANES, hk), 0)
    tiles = []
    for i in range(tb // SUBLANES):
        x = g[i * SUBLANES:(i + 1) * SUBLANES]
        for sh in (1, 2, 4):
            x = x + jnp.where(row8 >= sh, pltpu.roll(x, sh, axis=0), 0.0)
        if i % tpc:
            x = x + tiles[-1][SUBLANES - 1:SUBLANES, :]
        tiles.append(x)
    bc = jnp.concatenate(tiles, axis=0)
    bl = [tiles[(a + 1) * tpc - 1][SUBLANES - 1:SUBLANES, :] for a in range(tb // c)]
    bl_rows = jnp.concatenate([jnp.broadcast_to(b, (c, hk)) for b in bl], axis=0)

    q = q_b.astype(F32) * (GLA_HK ** -0.5)
    k = k_b.astype(F32)
    qd = q * jnp.exp(bc)
    kd = k * jnp.exp(-bc)
    ku = k * jnp.exp(bl_rows - bc)
    qd_b, kd_b, ku_b = qd.astype(BF16), kd.astype(BF16), ku.astype(BF16)
    ops_ref[_R_QD:_R_QD + tb] = qd_b
    ops_ref[_R_KD:_R_KD + tb] = kd_b
    ops_ref[_R_KU:_R_KU + tb] = ku_b

    def rows(t, a):
        return t[a * c:(a + 1) * c]

    def put(r0, a, val):
        ops_ref[r0 + a * c:r0 + (a + 1) * c] = val

    e0, e1, e2, e3 = (jnp.exp(b) for b in bl)
    e01 = jnp.exp(bl[0] + bl[1])
    e012 = jnp.exp(bl[0] + bl[1] + bl[2])
    e23 = jnp.exp(bl[2] + bl[3])
    e123 = jnp.exp(bl[1] + bl[2] + bl[3])
    dec_ref[...] = jnp.broadcast_to(jnp.exp(bl[0] + bl[1] + bl[2] + bl[3]), dec_ref.shape)
    put(_R_QX, 0, rows(qd_b, 0))
    put(_R_QX, 1, (rows(qd, 1) * e0).astype(BF16))
    put(_R_QX, 2, (rows(qd, 2) * e01).astype(BF16))
    put(_R_QX, 3, (rows(qd, 3) * e012).astype(BF16))
    put(_R_KX, 0, (rows(ku, 0) * e123).astype(BF16))
    put(_R_KX, 1, (rows(ku, 1) * e23).astype(BF16))
    put(_R_KX, 2, (rows(ku, 2) * e3).astype(BF16))
    put(_R_KX, 3, rows(ku_b, 3))
    put(_R_QHI, 0, rows(qd_b, 2))
    put(_R_QHI, 1, (rows(qd, 3) * e2).astype(BF16))
    put(_R_KLO, 0, (rows(ku, 0) * e1).astype(BF16))
    put(_R_KLO, 1, rows(ku_b, 1))
    put(_R_KB0, 0, rows(ku_b, 0))
    put(_R_KB0, 1, rows(kd_b, 1))
    put(_R_KB1, 0, rows(ku_b, 2))
    put(_R_KB1, 1, rows(kd_b, 3))


def _gla_mix(ops_ref, dec_ref, v, r_b, ng, st_ref):
    c = _GLA_C
    half = 2 * c

    def op(r0, n, a=0):
        return ops_ref[r0 + a * c:r0 + (a + n) * c]

    cross = lax.dot_general(op(_R_QHI, 2), op(_R_KLO, 2), NT_DIMS, preferred_element_type=F32).astype(BF16)
    ri = lax.broadcasted_iota(jnp.int32, (half, half), 0)
    ci = lax.broadcasted_iota(jnp.int32, (half, half), 1)
    causal = ri >= ci

    def half_attn(h):
        top = lax.dot_general(op(_R_QD, 1, 2 * h), op(_R_KD, 2, 2 * h), NT_DIMS, preferred_element_type=F32)
        bot = lax.dot_general(op(_R_QD, 1, 2 * h + 1), op(_R_KB1 if h else _R_KB0, 2), NT_DIMS,
                              preferred_element_type=F32)
        return jnp.where(causal, jnp.concatenate([top, bot], axis=0), 0.0).astype(BF16)

    st = st_ref[...]
    o_top = jnp.dot(half_attn(0), v[0:half], preferred_element_type=F32)
    o_bot = jnp.dot(jnp.concatenate([cross, half_attn(1)], axis=1), v, preferred_element_type=F32)
    o = jnp.concatenate([o_top, o_bot], axis=0)
    o = o + lax.dot_general(op(_R_QX, 4), st.astype(BF16), NT_DIMS, preferred_element_type=F32)
    st_ref[...] = st * dec_ref[0:1, :] + lax.dot_general(v, op(_R_KX, 4), TN_DIMS, preferred_element_type=F32)
    on = o * _rms_scale(o) * ng
    return on * _silu(r_b.astype(F32))


def _gla_kernel(x_ref, q_ref, k_ref, v_ref, r_ref, lr_ref, wg_ref, bg_ref, ng_ref, wout_ref, gpost_ref,
                o_ref, st_ref, ops_ref, dec_ref):
    @pl.when(pl.program_id(1) == 0)
    def _():
        st_ref[...] = jnp.zeros_like(st_ref)

    hk, hv = GLA_HK, GLA_HV
    nheads = st_ref.shape[0]
    lr, ng = lr_ref[...], ng_ref[...]

    def prep(j):
        ks = slice(j * hk, (j + 1) * hk)
        _gla_prep(q_ref[:, ks], k_ref[:, ks], lr, wg_ref[:, ks], bg_ref[:, ks], ops_ref.at[j], dec_ref.at[j])

    def mix(j):
        vs = slice(j * hv, (j + 1) * hv)
        out = _gla_mix(ops_ref.at[j], dec_ref.at[j], v_ref[:, vs], r_ref[:, vs], ng, st_ref.at[j])
        return jnp.dot(out.astype(BF16), wout_ref[vs, :], preferred_element_type=F32)

    prep(0)
    y = None
    for j in range(nheads):
        if j + 1 < nheads:
            prep(j + 1)
        part = mix(j)
        y = part if y is None else y + part
    o_ref[...] = x_ref[...] + y * _rms_scale(y) * gpost_ref[...]


def _gla_out(x, p, p_lr, wg, bg, ng, w_out_b, wi, norm_post, gi, tb=_GLA_TB):
    bsz, seq, d = x.shape
    nh = GLA_HEADS
    dk, dv = nh * GLA_HK, nh * GLA_HV

    def act(width, col):
        return pl.BlockSpec((None, tb, width), lambda b, t: (b, t, col))

    in_specs = [act(d, 0), act(dk, 0), act(dk, 1), act(dv, 2 * dk // dv), act(dv, (2 * dk + dv) // dv),
                act(LANES, 0), _const_spec((LANES, dk)), _const_spec((1, dk)), _const_spec((1, GLA_HV)),
                _layer_spec(w_out_b.shape[1:], wi), _layer_spec((1, d), gi)]
    wg_pad = jnp.pad(wg.astype(BF16), ((0, LANES - GLA_LOWRANK), (0, 0)))
    return pl.pallas_call(
        _gla_kernel, grid=(bsz, seq // tb), in_specs=in_specs,
        out_specs=act(d, 0),
        out_shape=jax.ShapeDtypeStruct((bsz, seq, d), F32),
        scratch_shapes=[pltpu.VMEM((nh, GLA_HV, GLA_HK), F32),
                        pltpu.VMEM((nh, _GLA_OPS_ROWS, GLA_HK), BF16),
                        pltpu.VMEM((nh, SUBLANES, GLA_HK), F32)],
        compiler_params=_cparams(("parallel", "arbitrary"), 48), name="gla_out",
    )(x, p, p, p, p, p_lr, wg_pad, bg.reshape(1, dk).astype(F32), ng.reshape(1, GLA_HV).astype(F32),
      w_out_b, norm_post.reshape(norm_post.shape[0], 1, d))


def _even_layer(x, i, norm_pre, norm_post, w_in_b, s5_ops, s5_d, w_glu_b, b_glu, cw, cb, lng, lnb, w_pw_b, b_pw,
                w_out_b):
    bsz, seq, d = x.shape
    j = i // 2
    wa = s5_d.shape[1]
    a_in, gates = _even_in(x.reshape(bsz * seq, d), norm_pre, i, w_in_b, j, wa)
    ya = _s5_apply(a_in.reshape(bsz, seq, wa), *s5_ops, s5_d, j)
    return _even_tail(x, ya, gates.reshape(bsz, seq, -1), i, w_glu_b, b_glu, cw, cb, lng, lnb,
                      w_pw_b, b_pw, w_out_b, norm_post)


def _odd_layer(x, i, norm_pre, norm_post, w_main_b, w_lr_b, wg, bg, ng, w_out_b):
    bsz, seq, d = x.shape
    j = i // 2
    n_main = w_main_b.shape[2] - GLA_LOWRANK
    p, p_lr = _norm_proj(x.reshape(bsz * seq, d), norm_pre, i, [w_main_b, w_lr_b], j,
                         [[n_main], [w_lr_b.shape[2]]], [BF16, BF16])
    return _gla_out(x, p.reshape(bsz, seq, -1), p_lr.reshape(bsz, seq, -1), wg, bg, ng, w_out_b, j, norm_post, i)


def kernel(x, norm_pre, norm_post, ev_w_in, s5_lambda_re, s5_lambda_im, s5_log_dt, s5_b_re, s5_b_im, s5_c_re, s5_c_im, s5_d, s5_w_glu, s5_b_glu, conv_w, conv_b, conv_ln_g, conv_ln_b, conv_w_pw, conv_b_pw, ev_w_out, od_w_in, gla_w_gate_up, gla_b_gate, gla_norm_g, od_w_out):
    depth = norm_pre.shape[0]
    norm_pre, norm_post = norm_pre.astype(F32), norm_post.astype(F32)
    ev_w_in_b, s5_w_glu_b, conv_w_pw_b = ev_w_in.astype(BF16), s5_w_glu.astype(BF16), conv_w_pw.astype(BF16)
    ev_w_out_b, od_w_out_b = ev_w_out.astype(BF16), od_w_out.astype(BF16)
    n_main = od_w_in.shape[2] - GLA_LOWRANK
    od_main_b = od_w_in.astype(BF16)
    od_lr_b = jnp.pad(od_w_in[:, :, n_main:], ((0, 0), (0, 0), (0, LANES - GLA_LOWRANK))).astype(BF16)
    s5_ops = _s5_prep(s5_lambda_re, s5_lambda_im, s5_log_dt, s5_b_re, s5_b_im, s5_c_re, s5_c_im)
    for i in range(depth):
        j = i // 2
        if i % 2 == 0:
            x = _even_layer(x, i, norm_pre, norm_post, ev_w_in_b, s5_ops, s5_d, s5_w_glu_b,
                            s5_b_glu, conv_w, conv_b, conv_ln_g, conv_ln_b, conv_w_pw_b, conv_b_pw, ev_w_out_b)
        else:
            x = _odd_layer(x, i, norm_pre, norm_post, od_main_b, od_lr_b, gla_w_gate_up[j], gla_b_gate[j],
                           gla_norm_g[j], od_w_out_b)
    return x
```

```python
import functools

import jax
import jax.numpy as jnp
from jax import lax
from jax.experimental import pallas as pl
from jax.experimental.pallas import tpu as pltpu

F32 = jnp.float32
BF16 = jnp.bfloat16

EPS = 1e-6
LANES = 128
SUBLANES = 8

S5_GROUP = 16
S5_STATE = 64
S5_T = 16
S5_GPB = LANES // S5_GROUP
S5_SB = S5_GPB * S5_STATE
CONV_K = 31
CONV_HALO = 32
GLA_HEADS = 4
GLA_HK = 256
GLA_HV = 512
GLA_LOWRANK = 16
GLA_TAU = 16.0
GLA_CHUNK = 64

NT_DIMS = (((1,), (1,)), ((), ()))
TN_DIMS = (((0,), (0,)), ((), ()))


def _cparams(sem, vmem_mb):
    return pltpu.CompilerParams(dimension_semantics=sem, vmem_limit_bytes=vmem_mb * 1024 * 1024)


def _const_spec(shape):
    nd = len(shape)
    return pl.BlockSpec(shape, lambda *_: (0,) * nd, pipeline_mode=pl.Buffered(1))


def _rms_scale(x):
    return lax.rsqrt(jnp.mean(x * x, axis=-1, keepdims=True) + EPS)


def _sigmoid(x):
    return 0.5 * jnp.tanh(0.5 * x) + 0.5


def _silu(x):
    return x * _sigmoid(x)


def _layer_spec(shape, layer):
    nd = len(shape)
    return pl.BlockSpec((None,) + tuple(shape), lambda *_: (layer,) + (0,) * nd, pipeline_mode=pl.Buffered(1))


def _norm_proj_kernel(x_ref, g_ref, *refs, n_w, splits, n_chunk):
    w_refs, o_refs = refs[:n_w], refs[n_w:]
    x = x_ref[...]
    u = (x * _rms_scale(x) * g_ref[...]).astype(BF16)
    outs = iter(o_refs)
    for w_ref, widths in zip(w_refs, splits):
        base = 0
        for width in widths:
            o_ref = next(outs)
            for n0 in range(0, width, n_chunk):
                n1 = min(n0 + n_chunk, width)
                o_ref[:, n0:n1] = jnp.dot(u, w_ref[:, base + n0:base + n1],
                                          preferred_element_type=F32).astype(o_ref.dtype)
            base += width


def _norm_proj(x2, g_stack, gi, w_stacks, wi, splits, out_dtypes, tm=512, n_chunk=512):
    m, d = x2.shape
    widths = [wd for ws in splits for wd in ws]
    in_specs = [pl.BlockSpec((tm, d), lambda i: (i, 0)), _layer_spec((1, d), gi)]
    in_specs += [_layer_spec((d, sum(ws)), wi) for ws in splits]
    out_specs = [pl.BlockSpec((tm, wd), lambda i: (i, 0)) for wd in widths]
    out_shape = [jax.ShapeDtypeStruct((m, wd), dt) for wd, dt in zip(widths, out_dtypes)]
    return pl.pallas_call(
        functools.partial(_norm_proj_kernel, n_w=len(w_stacks), splits=splits, n_chunk=n_chunk),
        grid=(m // tm,), in_specs=in_specs, out_specs=out_specs, out_shape=out_shape,
        compiler_params=_cparams(("parallel",), 56), name="norm_proj",
    )(x2, g_stack.reshape(g_stack.shape[0], 1, d), *w_stacks)


def _even_in_kernel(x_ref, g_ref, w_ref, a_ref, gates_ref, *, rows, n_chunk):
    tm, wa = a_ref.shape
    for r0 in range(0, tm, rows):
        rs = slice(r0, r0 + rows)
        x = x_ref[rs, :]
        u = (x * _rms_scale(x) * g_ref[...]).astype(BF16)

        def proj(part, n0):
            c0 = part * wa + n0
            return jnp.dot(u, w_ref[:, c0:c0 + n_chunk], preferred_element_type=F32)

        for n0 in range(0, wa, n_chunk):
            cols = slice(n0, n0 + n_chunk)
            a_ref[rs, cols] = proj(0, n0)
            gates_ref[rs, cols] = _silu(proj(1, n0)).astype(gates_ref.dtype)
            gates_ref[rs, wa + n0:wa + n0 + n_chunk] = (
                proj(2, n0) * _sigmoid(proj(3, n0))).astype(gates_ref.dtype)
            gates_ref[rs, 2 * wa + n0:2 * wa + n0 + n_chunk] = _silu(proj(4, n0)).astype(gates_ref.dtype)


def _even_in(x2, g_stack, gi, w_stack, wi, wa, tm=1024, rows=512, n_chunk=512):
    m, d = x2.shape
    assert w_stack.shape[2] == 5 * wa and wa % n_chunk == 0
    return pl.pallas_call(
        functools.partial(_even_in_kernel, rows=rows, n_chunk=n_chunk), grid=(m // tm,),
        in_specs=[pl.BlockSpec((tm, d), lambda i: (i, 0)), _layer_spec((1, d), gi),
                  _layer_spec(w_stack.shape[1:], wi)],
        out_specs=[pl.BlockSpec((tm, wa), lambda i: (i, 0)), pl.BlockSpec((tm, 3 * wa), lambda i: (i, 0))],
        out_shape=[jax.ShapeDtypeStruct((m, wa), F32), jax.ShapeDtypeStruct((m, 3 * wa), BF16)],
        compiler_params=_cparams(("parallel",), 56), name="even_in",
    )(x2, g_stack.reshape(g_stack.shape[0], 1, d), w_stack)


def _s5_prep_kernel(lre_ref, lim_ref, dt_ref, btr_ref, bti_ref, cr_ref, ci_ref,
                    g_ref, wz_ref, vt_ref, lp_ref, own_ref):
    lre, lim, dt = lre_ref[...], lim_ref[...], jnp.exp(dt_ref[...])

    def lam_pow(m):
        mag = jnp.exp(lre * dt * float(m))
        ang = lim * dt * float(m)
        return mag * jnp.cos(ang), mag * jnp.sin(ang)

    def per_channel(t):
        return jnp.concatenate([jnp.broadcast_to(t[g:g + 1], (S5_GROUP, S5_STATE)) for g in range(S5_GPB)],
                               axis=0)

    def lane_tile(t):
        t2 = jnp.concatenate([t, t], axis=1)
        return jnp.concatenate([t2] * (S5_SB // LANES), axis=1)

    pows_g = [lam_pow(m) for m in range(S5_T + 1)]
    pows = [(per_channel(pr), per_channel(pi)) for pr, pi in pows_g]
    nr, ni = pows_g[1][0] - 1.0, pows_g[1][1]
    den = lre * lre + lim * lim
    cfr = per_channel((nr * lre + ni * lim) / den)
    cfi = per_channel((ni * lre - nr * lim) / den)
    btr, bti = btr_ref[...], bti_ref[...]
    bbr = cfr * btr - cfi * bti
    bbi = cfr * bti + cfi * btr
    cr, ci = cr_ref[...], ci_ref[...]

    row_g = lax.broadcasted_iota(jnp.int32, (LANES, S5_SB), 0) // S5_GROUP
    lane_g = lax.broadcasted_iota(jnp.int32, (LANES, S5_SB), 1) // S5_STATE
    own_ref[...] = jnp.where(row_g == lane_g, 1.0, 0.0)

    def place(re, im):
        def own(t):
            return jnp.where(own_ref[...] != 0.0, lane_tile(t), 0.0)
        return jnp.concatenate([own(re), own(im)], axis=1).astype(BF16)

    def hi_lo(re, im):
        re_hi, im_hi = re.astype(BF16).astype(F32), im.astype(BF16).astype(F32)
        return place(re_hi, im_hi), place(re - re_hi, im - im_hi)

    def dot_nt(a, b):
        return lax.dot_general(a, b, NT_DIMS, preferred_element_type=F32)

    for j in range(S5_T - 1):
        pr, pi = pows[S5_T - 1 - j]
        wz_ref[j * LANES:(j + 1) * LANES, :] = place(pr * bbr - pi * bbi, pr * bbi + pi * bbr)
    bb_hi, bb_lo = hi_lo(bbr, bbi)
    wz_ref[(S5_T - 1) * LANES:, :] = bb_hi
    kts = []
    for tau in range(S5_T + 1):
        pr, pi = pows[tau]
        ct_hi, ct_lo = hi_lo(cr * pr - ci * pi, -(cr * pi + ci * pr))
        if tau >= 1:
            vt_ref[(tau - 1) * LANES:tau * LANES, :] = ct_hi
        if tau < S5_T:
            kts.append(dot_nt(bb_hi, ct_hi) + dot_nt(bb_hi, ct_lo) + dot_nt(bb_lo, ct_hi))
    zero = jnp.zeros((LANES, LANES), F32)
    for r in range(S5_T):
        for e in range(2):
            tau = S5_T - 2 - r + e
            kt = kts[tau] if tau >= 0 else zero
            g_ref[r * LANES:(r + 1) * LANES, e * LANES:(e + 1) * LANES] = kt.astype(g_ref.dtype)
    own_row = (lax.broadcasted_iota(jnp.int32, (S5_GPB, S5_SB), 0)
               == lax.broadcasted_iota(jnp.int32, (S5_GPB, S5_SB), 1) // S5_STATE)

    def as_row(t):
        return jnp.sum(jnp.where(own_row, lane_tile(t), 0.0), axis=0, keepdims=True)

    lp_ref[...] = jnp.zeros_like(lp_ref)
    for r in range(SUBLANES + 1):
        pr, pi = lam_pow(S5_T * r)
        lp_ref[r:r + 1, 0:S5_SB] = as_row(pr)
        lp_ref[r:r + 1, S5_SB:] = as_row(pi)


def _s5_prep(lam_re, lam_im, log_dt, b_re, b_im, c_re, c_im):
    nl, ng = lam_re.shape[:2]
    nblk = nl * ng // S5_GPB

    def groups(t):
        return t.astype(F32).reshape(nblk, S5_GPB, S5_STATE)

    def mats(t):
        return t.astype(F32).reshape(nblk, LANES, S5_STATE)

    dt_g = jnp.broadcast_to(log_dt.astype(F32)[:, :, None], (nl, ng, S5_STATE))
    ins = [groups(lam_re), groups(lam_im), groups(dt_g),
           mats(jnp.swapaxes(b_re, 2, 3)), mats(jnp.swapaxes(b_im, 2, 3)), mats(c_re), mats(c_im)]
    grp_spec = pl.BlockSpec((None, S5_GPB, S5_STATE), lambda k: (k, 0, 0))
    mat_spec = pl.BlockSpec((None, LANES, S5_STATE), lambda k: (k, 0, 0))
    kt = S5_T * LANES
    out_shape = [jax.ShapeDtypeStruct((nblk, kt, 2 * LANES), BF16),
                 jax.ShapeDtypeStruct((nblk, kt, 2 * S5_SB), BF16),
                 jax.ShapeDtypeStruct((nblk, kt, 2 * S5_SB), BF16),
                 jax.ShapeDtypeStruct((nblk, 2 * SUBLANES, 2 * S5_SB), F32)]
    out_specs = [pl.BlockSpec((None,) + s.shape[1:], lambda k: (k, 0, 0)) for s in out_shape]
    return pl.pallas_call(
        _s5_prep_kernel, grid=(nblk,),
        in_specs=[grp_spec] * 3 + [mat_spec] * 4, out_specs=out_specs, out_shape=out_shape,
        scratch_shapes=[pltpu.VMEM((LANES, S5_SB), F32)],
        compiler_params=_cparams(("parallel",), 48), name="s5_prep",
    )(*ins)


def _s5_kernel(u_ref, g_ref, wz_ref, vt_ref, lp_ref, d_ref, o_ref, u16_ref, z_ref, s_ref, y_ref):
    nc = u16_ref.shape[0]
    kq = S5_T * LANES // 4
    z = None
    for q in range(4):
        for j in range(q * S5_T // 4, (q + 1) * S5_T // 4):
            u16_ref[:, j * LANES:(j + 1) * LANES] = u_ref[pl.ds(j, nc, stride=S5_T), :].astype(BF16)
        zq = jnp.dot(u16_ref[:, q * kq:(q + 1) * kq], wz_ref[q * kq:(q + 1) * kq, :],
                     preferred_element_type=F32)
        z = zq if z is None else z + zq
    z_ref[...] = z

    row = lax.broadcasted_iota(jnp.int32, (SUBLANES, S5_SB), 0)

    def shift(x, sh):
        return jnp.where(row >= sh, pltpu.roll(x, sh, axis=0), 0.0)

    def lp(r0, r1):
        return lp_ref[r0:r1, 0:S5_SB], lp_ref[r0:r1, S5_SB:]

    def scan_rows(r0, carry):
        cr, ci = carry
        xr = z_ref[r0:r0 + SUBLANES, 0:S5_SB]
        xi = z_ref[r0:r0 + SUBLANES, S5_SB:]
        for sh in (1, 2, 4):
            pr, pi = lp(sh, sh + 1)
            sr, si = shift(xr, sh), shift(xi, sh)
            xr, xi = xr + pr * sr - pi * si, xi + pr * si + pi * sr
        pr, pi = lp(0, SUBLANES)
        s_ref[r0:r0 + SUBLANES, 0:S5_SB] = (pr * cr - pi * ci + shift(xr, 1)).astype(s_ref.dtype)
        s_ref[r0:r0 + SUBLANES, S5_SB:] = (pr * ci + pi * cr + shift(xi, 1)).astype(s_ref.dtype)
        pr, pi = lp(SUBLANES, SUBLANES + 1)
        last = SUBLANES - 1
        return (pr * cr - pi * ci + xr[last:last + 1], pr * ci + pi * cr + xi[last:last + 1])

    pairs = S5_T // 2
    for ip in range(pairs):
        kk = (2 * ip + 2) * LANES
        y_ref[:, 2 * ip * LANES:(2 * ip + 2) * LANES] = jnp.dot(
            u16_ref[:, 0:kk], g_ref[(S5_T - 2 - 2 * ip) * LANES:, :], preferred_element_type=F32)

    carry = (jnp.zeros((1, S5_SB), F32), jnp.zeros((1, S5_SB), F32))
    for r0 in range(0, nc, SUBLANES):
        carry = scan_rows(r0, carry)

    sb = s_ref[...].astype(BF16)
    d = d_ref[...]
    for ip in range(pairs):
        acc = y_ref[:, 2 * ip * LANES:(2 * ip + 2) * LANES] + lax.dot_general(
            sb, vt_ref[2 * ip * LANES:(2 * ip + 2) * LANES, :], NT_DIMS, preferred_element_type=F32)
        for e in range(2):
            i = 2 * ip + e
            ui = u_ref[pl.ds(i, nc, stride=S5_T), :]
            o_ref[pl.ds(i, nc, stride=S5_T), :] = acc[:, e * LANES:(e + 1) * LANES] + d * ui


def _s5_apply(a_in, gm, wz, vt, lp, d, layer):
    bsz, seq, width = a_in.shape
    nblk = width // LANES
    nc = seq // S5_T
    kt = S5_T * LANES
    act_spec = pl.BlockSpec((None, seq, LANES), lambda k, b: (b, 0, k))

    def wspec(shape):
        return pl.BlockSpec((None,) + shape, lambda k, b: (layer * nblk + k, 0, 0))

    return pl.pallas_call(
        _s5_kernel, grid=(nblk, bsz),
        in_specs=[act_spec, wspec((kt, 2 * LANES)), wspec((kt, 2 * S5_SB)), wspec((kt, 2 * S5_SB)),
                  wspec((2 * SUBLANES, 2 * S5_SB)), pl.BlockSpec((None, 1, LANES), lambda k, b: (layer, 0, k))],
        out_specs=act_spec,
        out_shape=jax.ShapeDtypeStruct((bsz, seq, width), F32),
        scratch_shapes=[pltpu.VMEM((nc, kt), BF16), pltpu.VMEM((nc, 2 * S5_SB), F32),
                        pltpu.VMEM((nc, 2 * S5_SB), F32), pltpu.VMEM((nc, kt), F32)],
        compiler_params=_cparams(("parallel", "parallel"), 56), name="s5_apply",
    )(a_in, gm, wz, vt, lp, d.astype(F32).reshape(d.shape[0], 1, width))


def _even_tail_kernel(x_ref, ya_ref, sa_ref, hin_ref, sb_ref,
                      wglu_ref, bglu_ref, cw_ref, cb_ref, lng_ref, lnb_ref,
                      wpw_ref, bpw_ref, wout_ref, gpost_ref, o_ref, h_ref, hs_ref, c_ref, *, rows):
    tm, w = c_ref.shape
    nsh = hs_ref.shape[1]

    @pl.when(pl.program_id(1) == 0)
    def _():
        h_ref[0:CONV_HALO, :] = jnp.zeros((CONV_HALO, w), F32)

    h_ref[CONV_HALO:, :] = hin_ref[...].astype(F32)
    for s in range(1, SUBLANES):
        hs_ref[s - 1] = h_ref[s:s + nsh, :]

    off = CONV_HALO - (CONV_K - 1)

    def conv_rows(ri, _):
        r0 = pl.multiple_of(ri * rows, rows)
        for c0 in range(0, w, LANES):
            acc = jnp.broadcast_to(cb_ref[:, c0:c0 + LANES], (rows, LANES))
            for s in range(SUBLANES):
                taps = [o for o in range(off, off + CONV_K) if o % SUBLANES == s]
                src = h_ref if s == 0 else hs_ref.at[s - 1]
                lo, hi = taps[0] - s, taps[-1] - s + rows
                win = src[pl.ds(pl.multiple_of(r0 + lo, SUBLANES), hi - lo), c0:c0 + LANES]
                for o in taps:
                    k = o - off
                    acc = acc + cw_ref[k:k + 1, c0:c0 + LANES] * win[o - s - lo:o - s - lo + rows]
            c_ref[pl.ds(r0, rows), c0:c0 + LANES] = acc
        return 0

    lax.fori_loop(0, tm // rows, conv_rows, 0)
    h_ref[0:CONV_HALO, :] = h_ref[tm:tm + CONV_HALO, :]

    hc = c_ref[...]
    mu = jnp.mean(hc, axis=-1, keepdims=True)
    xc = hc - mu
    hn = xc * lax.rsqrt(jnp.mean(xc * xc, axis=-1, keepdims=True) + EPS) * lng_ref[...] + lnb_ref[...]
    yb = jnp.dot(_silu(hn).astype(BF16), wpw_ref[...], preferred_element_type=F32) + bpw_ref[...]
    yb = yb * sb_ref[...].astype(F32)

    ya = jax.nn.gelu(ya_ref[...])
    gate = jnp.dot(ya.astype(BF16), wglu_ref[...], preferred_element_type=F32) + bglu_ref[...]
    ya = ya * _sigmoid(gate) * sa_ref[...].astype(F32)

    y = jnp.dot(ya.astype(BF16), wout_ref[0:w, :], preferred_element_type=F32)
    y = y + jnp.dot(yb.astype(BF16), wout_ref[w:, :], preferred_element_type=F32)
    o_ref[...] = x_ref[...] + y * _rms_scale(y) * gpost_ref[...]


def _even_tail(x, ya, p, i, wglu_b, bglu, cw, cb, lng, lnb, wpw_b, bpw, wout_b, norm_post, tm=512, rows=32):
    bsz, seq, d = x.shape
    w = ya.shape[-1]
    j = i // 2

    def act(col):
        return pl.BlockSpec((None, tm, w), lambda b, t: (b, t, col))

    vec = lambda v: v.astype(F32).reshape(v.shape[0], 1, v.shape[1])
    cw_pad = jnp.pad(cw.astype(F32), ((0, 0), (0, CONV_HALO - CONV_K), (0, 0)))
    consts = [wglu_b, vec(bglu), cw_pad, vec(cb), vec(lng), vec(lnb), wpw_b, vec(bpw), wout_b]
    const_specs = [_layer_spec(c.shape[1:], j) for c in consts] + [_layer_spec((1, d), i)]
    consts.append(vec(norm_post))
    return pl.pallas_call(
        functools.partial(_even_tail_kernel, rows=rows), grid=(bsz, seq // tm),
        in_specs=[act(0), act(0), act(0), act(1), act(2)] + const_specs,
        out_specs=act(0),
        out_shape=jax.ShapeDtypeStruct((bsz, seq, d), F32),
        scratch_shapes=[pltpu.VMEM((CONV_HALO + tm, w), F32),
                        pltpu.VMEM((SUBLANES - 1, CONV_HALO + tm - SUBLANES, w), F32),
                        pltpu.VMEM((tm, w), F32)],
        compiler_params=_cparams(("parallel", "arbitrary"), 56), name="even_tail",
    )(x, ya, p, p, p, *consts)


_GLA_C = GLA_CHUNK
_GLA_TB = 4 * _GLA_C
_R_QD, _R_KD, _R_KU, _R_QX, _R_KX = (i * _GLA_TB for i in range(5))
_R_QHI, _R_KLO, _R_KB0, _R_KB1 = (5 * _GLA_TB + i * 2 * _GLA_C for i in range(4))
_GLA_OPS_ROWS = 7 * _GLA_TB


def _gla_prep(q_b, k_b, lr, wg, bg, ops_ref, dec_ref):
    tb, hk = q_b.shape
    c = _GLA_C
    assert tb == _GLA_TB
    tpc = c // SUBLANES

    gp = jnp.dot(lr, wg, preferred_element_type=F32) + bg
    g = (jnp.minimum(gp, 0.0) - jnp.log(1.0 + jnp.exp(-jnp.abs(gp)))) * (1.0 / GLA_TAU)

    row8 = lax.broadcasted_iota(jnp.int32, (SUBLANES, hk), 0)
    tiles = []
    for i in range(tb // SUBLANES):
        x = g[i * SUBLANES:(i + 1) * SUBLANES]
        for sh in (1, 2, 4):
            x = x + jnp.where(row8 >= sh, pltpu.roll(x, sh, axis=0), 0.0)
        if i % tpc:
            x = x + tiles[-1][SUBLANES - 1:SUBLANES, :]
        tiles.append(x)
    bc = jnp.concatenate(tiles, axis=0)
    bl = [tiles[(a + 1) * tpc - 1][SUBLANES - 1:SUBLANES, :] for a in range(tb // c)]
    bl_rows = jnp.concatenate([jnp.broadcast_to(b, (c, hk)) for b in bl], axis=0)

    q = q_b.astype(F32) * (GLA_HK ** -0.5)
    k = k_b.astype(F32)
    qd = q * jnp.exp(bc)
    kd = k * jnp.exp(-bc)
    ku = k * jnp.exp(bl_rows - bc)
    qd_b, kd_b, ku_b = qd.astype(BF16), kd.astype(BF16), ku.astype(BF16)
    ops_ref[_R_QD:_R_QD + tb] = qd_b
    ops_ref[_R_KD:_R_KD + tb] = kd_b
    ops_ref[_R_KU:_R_KU + tb] = ku_b

    def rows(t, a):
        return t[a * c:(a + 1) * c]

    def put(r0, a, val):
        ops_ref[r0 + a * c:r0 + (a + 1) * c] = val

    e0, e1, e2, e3 = (jnp.exp(b) for b in bl)
    e01 = jnp.exp(bl[0] + bl[1])
    e012 = jnp.exp(bl[0] + bl[1] + bl[2])
    e23 = jnp.exp(bl[2] + bl[3])
    e123 = jnp.exp(bl[1] + bl[2] + bl[3])
    dec_ref[...] = jnp.broadcast_to(jnp.exp(bl[0] + bl[1] + bl[2] + bl[3]), dec_ref.shape)
    put(_R_QX, 0, rows(qd_b, 0))
    put(_R_QX, 1, (rows(qd, 1) * e0).astype(BF16))
    put(_R_QX, 2, (rows(qd, 2) * e01).astype(BF16))
    put(_R_QX, 3, (rows(qd, 3) * e012).astype(BF16))
    put(_R_KX, 0, (rows(ku, 0) * e123).astype(BF16))
    put(_R_KX, 1, (rows(ku, 1) * e23).astype(BF16))
    put(_R_KX, 2, (rows(ku, 2) * e3).astype(BF16))
    put(_R_KX, 3, rows(ku_b, 3))
    put(_R_QHI, 0, rows(qd_b, 2))
    put(_R_QHI, 1, (rows(qd, 3) * e2).astype(BF16))
    put(_R_KLO, 0, (rows(ku, 0) * e1).astype(BF16))
    put(_R_KLO, 1, rows(ku_b, 1))
    put(_R_KB0, 0, rows(ku_b, 0))
    put(_R_KB0, 1, rows(kd_b, 1))
    put(_R_KB1, 0, rows(ku_b, 2))
    put(_R_KB1, 1, rows(kd_b, 3))


def _gla_mix(ops_ref, dec_ref, v, r_b, ng, st_ref):
    c = _GLA_C
    half = 2 * c

    def op(r0, n, a=0):
        return ops_ref[r0 + a * c:r0 + (a + n) * c]

    cross = lax.dot_general(op(_R_QHI, 2), op(_R_KLO, 2), NT_DIMS, preferred_element_type=F32).astype(BF16)
    ri = lax.broadcasted_iota(jnp.int32, (half, half), 0)
    ci = lax.broadcasted_iota(jnp.int32, (half, half), 1)
    causal = ri >= ci

    def half_attn(h):
        top = lax.dot_general(op(_R_QD, 1, 2 * h), op(_R_KD, 2, 2 * h), NT_DIMS, preferred_element_type=F32)
        bot = lax.dot_general(op(_R_QD, 1, 2 * h + 1), op(_R_KB1 if h else _R_KB0, 2), NT_DIMS,
                              preferred_element_type=F32)
        return jnp.where(causal, jnp.concatenate([top, bot], axis=0), 0.0).astype(BF16)

    st = st_ref[...]
    o_top = jnp.dot(half_attn(0), v[0:half], preferred_element_type=F32)
    o_bot = jnp.dot(jnp.concatenate([cross, half_attn(1)], axis=1), v, preferred_element_type=F32)
    o = jnp.concatenate([o_top, o_bot], axis=0)
    o = o + lax.dot_general(op(_R_QX, 4), st.astype(BF16), NT_DIMS, preferred_element_type=F32)
    st_ref[...] = st * dec_ref[0:1, :] + lax.dot_general(v, op(_R_KX, 4), TN_DIMS, preferred_element_type=F32)
    on = o * _rms_scale(o) * ng
    return on * _silu(r_b.astype(F32))


def _gla_kernel(x_ref, q_ref, k_ref, v_ref, r_ref, lr_ref, wg_ref, bg_ref, ng_ref, wout_ref, gpost_ref,
                o_ref, st_ref, ops_ref, dec_ref):
    @pl.when(pl.program_id(1) == 0)
    def _():
        st_ref[...] = jnp.zeros_like(st_ref)

    hk, hv = GLA_HK, GLA_HV
    nheads = st_ref.shape[0]
    ng = ng_ref[...]
    items = [(b, j) for b in range(x_ref.shape[0] // _GLA_TB) for j in range(nheads)]

    def prep(b, j):
        rs, ks = slice(b * _GLA_TB, (b + 1) * _GLA_TB), slice(j * hk, (j + 1) * hk)
        _gla_prep(q_ref[rs, ks], k_ref[rs, ks], lr_ref[rs, :], wg_ref[:, ks], bg_ref[:, ks],
                  ops_ref.at[j], dec_ref.at[j])

    def mix(b, j):
        rs, vs = slice(b * _GLA_TB, (b + 1) * _GLA_TB), slice(j * hv, (j + 1) * hv)
        out = _gla_mix(ops_ref.at[j], dec_ref.at[j], v_ref[rs, vs], r_ref[rs, vs], ng, st_ref.at[j])
        return jnp.dot(out.astype(BF16), wout_ref[vs, :], preferred_element_type=F32)

    prep(*items[0])
    y = None
    for n, (b, j) in enumerate(items):
        if n + 1 < len(items):
            prep(*items[n + 1])
        part = mix(b, j)
        y = part if j == 0 else y + part
        if j == nheads - 1:
            rs = slice(b * _GLA_TB, (b + 1) * _GLA_TB)
            o_ref[rs, :] = x_ref[rs, :] + y * _rms_scale(y) * gpost_ref[...]


def _gla_out(x, p, p_lr, wg, bg, ng, w_out_b, wi, norm_post, gi, tb=2 * _GLA_TB):
    bsz, seq, d = x.shape
    nh = GLA_HEADS
    dk, dv = nh * GLA_HK, nh * GLA_HV

    def act(width, col):
        return pl.BlockSpec((None, tb, width), lambda b, t: (b, t, col))

    in_specs = [act(d, 0), act(dk, 0), act(dk, 1), act(dv, 2 * dk // dv), act(dv, (2 * dk + dv) // dv),
                act(LANES, 0), _const_spec((LANES, dk)), _const_spec((1, dk)), _const_spec((1, GLA_HV)),
                _layer_spec(w_out_b.shape[1:], wi), _layer_spec((1, d), gi)]
    wg_pad = jnp.pad(wg.astype(BF16), ((0, LANES - GLA_LOWRANK), (0, 0)))
    return pl.pallas_call(
        _gla_kernel, grid=(bsz, seq // tb), in_specs=in_specs,
        out_specs=act(d, 0),
        out_shape=jax.ShapeDtypeStruct((bsz, seq, d), F32),
        scratch_shapes=[pltpu.VMEM((nh, GLA_HV, GLA_HK), F32),
                        pltpu.VMEM((nh, _GLA_OPS_ROWS, GLA_HK), BF16),
                        pltpu.VMEM((nh, SUBLANES, GLA_HK), F32)],
        compiler_params=_cparams(("parallel", "arbitrary"), 48), name="gla_out",
    )(x, p, p, p, p, p_lr, wg_pad, bg.reshape(1, dk).astype(F32), ng.reshape(1, GLA_HV).astype(F32),
      w_out_b, norm_post.reshape(norm_post.shape[0], 1, d))


def _even_layer(x, i, norm_pre, norm_post, w_in_b, s5_ops, s5_d, w_glu_b, b_glu, cw, cb, lng, lnb, w_pw_b, b_pw,
                w_out_b):
    bsz, seq, d = x.shape
    j = i // 2
    wa = s5_d.shape[1]
    a_in, gates = _even_in(x.reshape(bsz * seq, d), norm_pre, i, w_in_b, j, wa)
    ya = _s5_apply(a_in.reshape(bsz, seq, wa), *s5_ops, s5_d, j)
    return _even_tail(x, ya, gates.reshape(bsz, seq, -1), i, w_glu_b, b_glu, cw, cb, lng, lnb,
                      w_pw_b, b_pw, w_out_b, norm_post)


def _odd_layer(x, i, norm_pre, norm_post, w_main_b, w_lr_b, wg, bg, ng, w_out_b):
    bsz, seq, d = x.shape
    j = i // 2
    n_main = w_main_b.shape[2] - GLA_LOWRANK
    p, p_lr = _norm_proj(x.reshape(bsz * seq, d), norm_pre, i, [w_main_b, w_lr_b], j,
                         [[n_main], [w_lr_b.shape[2]]], [BF16, BF16])
    return _gla_out(x, p.reshape(bsz, seq, -1), p_lr.reshape(bsz, seq, -1), wg, bg, ng, w_out_b, j, norm_post, i)


def kernel(x, norm_pre, norm_post, ev_w_in, s5_lambda_re, s5_lambda_im, s5_log_dt, s5_b_re, s5_b_im, s5_c_re, s5_c_im, s5_d, s5_w_glu, s5_b_glu, conv_w, conv_b, conv_ln_g, conv_ln_b, conv_w_pw, conv_b_pw, ev_w_out, od_w_in, gla_w_gate_up, gla_b_gate, gla_norm_g, od_w_out):
    depth = norm_pre.shape[0]
    norm_pre, norm_post = norm_pre.astype(F32), norm_post.astype(F32)
    ev_w_in_b, s5_w_glu_b, conv_w_pw_b = ev_w_in.astype(BF16), s5_w_glu.astype(BF16), conv_w_pw.astype(BF16)
    ev_w_out_b, od_w_out_b = ev_w_out.astype(BF16), od_w_out.astype(BF16)
    n_main = od_w_in.shape[2] - GLA_LOWRANK
    od_main_b = od_w_in.astype(BF16)
    od_lr_b = jnp.pad(od_w_in[:, :, n_main:], ((0, 0), (0, 0), (0, LANES - GLA_LOWRANK))).astype(BF16)
    s5_ops = _s5_prep(s5_lambda_re, s5_lambda_im, s5_log_dt, s5_b_re, s5_b_im, s5_c_re, s5_c_im)
    for i in range(depth):
        j = i // 2
        if i % 2 == 0:
            x = _even_layer(x, i, norm_pre, norm_post, ev_w_in_b, s5_ops, s5_d, s5_w_glu_b,
                            s5_b_glu, conv_w, conv_b, conv_ln_g, conv_ln_b, conv_w_pw_b, conv_b_pw, ev_w_out_b)
        else:
            x = _odd_layer(x, i, norm_pre, norm_post, od_main_b, od_lr_b, gla_w_gate_up[j], gla_b_gate[j],
                           gla_norm_g[j], od_w_out_b)
    return x
```

```python
import functools

import jax
import jax.numpy as jnp
from jax import lax
from jax.experimental import pallas as pl
from jax.experimental.pallas import tpu as pltpu

F32 = jnp.float32
BF16 = jnp.bfloat16

EPS = 1e-6
LANES = 128
SUBLANES = 8

S5_GROUP = 16
S5_STATE = 64
S5_T = 16
S5_GPB = LANES // S5_GROUP
S5_SB = S5_GPB * S5_STATE
CONV_K = 31
CONV_HALO = 32
GLA_HEADS = 4
GLA_HK = 256
GLA_HV = 512
GLA_LOWRANK = 16
GLA_TAU = 16.0
GLA_CHUNK = 64

NT_DIMS = (((1,), (1,)), ((), ()))
TN_DIMS = (((0,), (0,)), ((), ()))


def _cparams(sem, vmem_mb):
    return pltpu.CompilerParams(dimension_semantics=sem, vmem_limit_bytes=vmem_mb * 1024 * 1024)


def _const_spec(shape):
    nd = len(shape)
    return pl.BlockSpec(shape, lambda *_: (0,) * nd, pipeline_mode=pl.Buffered(1))


def _rms_scale(x):
    return lax.rsqrt(jnp.mean(x * x, axis=-1, keepdims=True) + EPS)


def _sigmoid(x):
    return 0.5 * jnp.tanh(0.5 * x) + 0.5


def _silu(x):
    h = 0.5 * x
    return h + h * jnp.tanh(h)


def _gelu_tanh(x):
    c = 0.7978845608028654
    h = 0.5 * x
    return h + h * jnp.tanh(x * (c + (c * 0.044715) * (x * x)))


def _layer_spec(shape, layer):
    nd = len(shape)
    return pl.BlockSpec((None,) + tuple(shape), lambda *_: (layer,) + (0,) * nd, pipeline_mode=pl.Buffered(1))


def _norm_proj_kernel(x_ref, g_ref, *refs, n_w, splits, n_chunk):
    w_refs, o_refs = refs[:n_w], refs[n_w:]
    x = x_ref[...]
    u = (x * _rms_scale(x) * g_ref[...]).astype(BF16)
    outs = iter(o_refs)
    for w_ref, widths in zip(w_refs, splits):
        base = 0
        for width in widths:
            o_ref = next(outs)
            for n0 in range(0, width, n_chunk):
                n1 = min(n0 + n_chunk, width)
                o_ref[:, n0:n1] = jnp.dot(u, w_ref[:, base + n0:base + n1],
                                          preferred_element_type=F32).astype(o_ref.dtype)
            base += width


def _norm_proj(x2, g_stack, gi, w_stacks, wi, splits, out_dtypes, tm=512, n_chunk=512):
    m, d = x2.shape
    widths = [wd for ws in splits for wd in ws]
    in_specs = [pl.BlockSpec((tm, d), lambda i: (i, 0)), _layer_spec((1, d), gi)]
    in_specs += [_layer_spec((d, sum(ws)), wi) for ws in splits]
    out_specs = [pl.BlockSpec((tm, wd), lambda i: (i, 0)) for wd in widths]
    out_shape = [jax.ShapeDtypeStruct((m, wd), dt) for wd, dt in zip(widths, out_dtypes)]
    return pl.pallas_call(
        functools.partial(_norm_proj_kernel, n_w=len(w_stacks), splits=splits, n_chunk=n_chunk),
        grid=(m // tm,), in_specs=in_specs, out_specs=out_specs, out_shape=out_shape,
        compiler_params=_cparams(("parallel",), 56), name="norm_proj",
    )(x2, g_stack.reshape(g_stack.shape[0], 1, d), *w_stacks)


def _even_in_kernel(x_ref, g_ref, w_ref, a_ref, gates_ref, *, rows, n_chunk):
    tm, wa = a_ref.shape
    for r0 in range(0, tm, rows):
        rs = slice(r0, r0 + rows)
        x = x_ref[rs, :]
        u = (x * _rms_scale(x) * g_ref[...]).astype(BF16)

        def proj(part, n0):
            c0 = part * wa + n0
            return jnp.dot(u, w_ref[:, c0:c0 + n_chunk], preferred_element_type=F32)

        for n0 in range(0, wa, n_chunk):
            cols = slice(n0, n0 + n_chunk)
            a_ref[rs, cols] = proj(0, n0)
            gates_ref[rs, cols] = _silu(proj(1, n0)).astype(gates_ref.dtype)
            gates_ref[rs, wa + n0:wa + n0 + n_chunk] = (
                proj(2, n0) * _sigmoid(proj(3, n0))).astype(gates_ref.dtype)
            gates_ref[rs, 2 * wa + n0:2 * wa + n0 + n_chunk] = _silu(proj(4, n0)).astype(gates_ref.dtype)


def _even_in(x2, g_stack, gi, w_stack, wi, wa, tm=1024, rows=512, n_chunk=512):
    m, d = x2.shape
    assert w_stack.shape[2] == 5 * wa and wa % n_chunk == 0
    return pl.pallas_call(
        functools.partial(_even_in_kernel, rows=rows, n_chunk=n_chunk), grid=(m // tm,),
        in_specs=[pl.BlockSpec((tm, d), lambda i: (i, 0)), _layer_spec((1, d), gi),
                  _layer_spec(w_stack.shape[1:], wi)],
        out_specs=[pl.BlockSpec((tm, wa), lambda i: (i, 0)), pl.BlockSpec((tm, 3 * wa), lambda i: (i, 0))],
        out_shape=[jax.ShapeDtypeStruct((m, wa), F32), jax.ShapeDtypeStruct((m, 3 * wa), BF16)],
        compiler_params=_cparams(("parallel",), 56), name="even_in",
    )(x2, g_stack.reshape(g_stack.shape[0], 1, d), w_stack)


def _s5_prep_kernel(lre_ref, lim_ref, dt_ref, btr_ref, bti_ref, cr_ref, ci_ref,
                    g_ref, wz_ref, vt_ref, lp_ref, own_ref):
    lre, lim, dt = lre_ref[...], lim_ref[...], jnp.exp(dt_ref[...])

    def lam_pow(m):
        mag = jnp.exp(lre * dt * float(m))
        ang = lim * dt * float(m)
        return mag * jnp.cos(ang), mag * jnp.sin(ang)

    def per_channel(t):
        return jnp.concatenate([jnp.broadcast_to(t[g:g + 1], (S5_GROUP, S5_STATE)) for g in range(S5_GPB)],
                               axis=0)

    def lane_tile(t):
        t2 = jnp.concatenate([t, t], axis=1)
        return jnp.concatenate([t2] * (S5_SB // LANES), axis=1)

    pows_g = [lam_pow(m) for m in range(S5_T + 1)]
    pows = [(per_channel(pr), per_channel(pi)) for pr, pi in pows_g]
    nr, ni = pows_g[1][0] - 1.0, pows_g[1][1]
    den = lre * lre + lim * lim
    cfr = per_channel((nr * lre + ni * lim) / den)
    cfi = per_channel((ni * lre - nr * lim) / den)
    btr, bti = btr_ref[...], bti_ref[...]
    bbr = cfr * btr - cfi * bti
    bbi = cfr * bti + cfi * btr
    cr, ci = cr_ref[...], ci_ref[...]

    row_g = lax.broadcasted_iota(jnp.int32, (LANES, S5_SB), 0) // S5_GROUP
    lane_g = lax.broadcasted_iota(jnp.int32, (LANES, S5_SB), 1) // S5_STATE
    own_ref[...] = jnp.where(row_g == lane_g, 1.0, 0.0)

    def place(re, im):
        def own(t):
            return jnp.where(own_ref[...] != 0.0, lane_tile(t), 0.0)
        return jnp.concatenate([own(re), own(im)], axis=1).astype(BF16)

    def hi_lo(re, im):
        re_hi, im_hi = re.astype(BF16).astype(F32), im.astype(BF16).astype(F32)
        return place(re_hi, im_hi), place(re - re_hi, im - im_hi)

    def dot_nt(a, b):
        return lax.dot_general(a, b, NT_DIMS, preferred_element_type=F32)

    for j in range(S5_T - 1):
        pr, pi = pows[S5_T - 1 - j]
        wz_ref[j * LANES:(j + 1) * LANES, :] = place(pr * bbr - pi * bbi, pr * bbi + pi * bbr)
    bb_hi, bb_lo = hi_lo(bbr, bbi)
    wz_ref[(S5_T - 1) * LANES:, :] = bb_hi
    kts = []
    for tau in range(S5_T + 1):
        pr, pi = pows[tau]
        ct_hi, ct_lo = hi_lo(cr * pr - ci * pi, -(cr * pi + ci * pr))
        if tau >= 1:
            vt_ref[(tau - 1) * LANES:tau * LANES, :] = ct_hi
        if tau < S5_T:
            kts.append(dot_nt(bb_hi, ct_hi) + dot_nt(bb_hi, ct_lo) + dot_nt(bb_lo, ct_hi))
    zero = jnp.zeros((LANES, LANES), F32)
    for r in range(S5_T):
        for e in range(2):
            tau = S5_T - 2 - r + e
            kt = kts[tau] if tau >= 0 else zero
            g_ref[r * LANES:(r + 1) * LANES, e * LANES:(e + 1) * LANES] = kt.astype(g_ref.dtype)
    own_row = (lax.broadcasted_iota(jnp.int32, (S5_GPB, S5_SB), 0)
               == lax.broadcasted_iota(jnp.int32, (S5_GPB, S5_SB), 1) // S5_STATE)

    def as_row(t):
        return jnp.sum(jnp.where(own_row, lane_tile(t), 0.0), axis=0, keepdims=True)

    lp_ref[...] = jnp.zeros_like(lp_ref)
    for r in range(SUBLANES + 1):
        pr, pi = lam_pow(S5_T * r)
        lp_ref[r:r + 1, 0:S5_SB] = as_row(pr)
        lp_ref[r:r + 1, S5_SB:] = as_row(pi)


def _s5_prep(lam_re, lam_im, log_dt, b_re, b_im, c_re, c_im):
    nl, ng = lam_re.shape[:2]
    nblk = nl * ng // S5_GPB

    def groups(t):
        return t.astype(F32).reshape(nblk, S5_GPB, S5_STATE)

    def mats(t):
        return t.astype(F32).reshape(nblk, LANES, S5_STATE)

    dt_g = jnp.broadcast_to(log_dt.astype(F32)[:, :, None], (nl, ng, S5_STATE))
    ins = [groups(lam_re), groups(lam_im), groups(dt_g),
           mats(jnp.swapaxes(b_re, 2, 3)), mats(jnp.swapaxes(b_im, 2, 3)), mats(c_re), mats(c_im)]
    grp_spec = pl.BlockSpec((None, S5_GPB, S5_STATE), lambda k: (k, 0, 0))
    mat_spec = pl.BlockSpec((None, LANES, S5_STATE), lambda k: (k, 0, 0))
    kt = S5_T * LANES
    out_shape = [jax.ShapeDtypeStruct((nblk, kt, 2 * LANES), BF16),
                 jax.ShapeDtypeStruct((nblk, kt, 2 * S5_SB), BF16),
                 jax.ShapeDtypeStruct((nblk, kt, 2 * S5_SB), BF16),
                 jax.ShapeDtypeStruct((nblk, 2 * SUBLANES, 2 * S5_SB), F32)]
    out_specs = [pl.BlockSpec((None,) + s.shape[1:], lambda k: (k, 0, 0)) for s in out_shape]
    return pl.pallas_call(
        _s5_prep_kernel, grid=(nblk,),
        in_specs=[grp_spec] * 3 + [mat_spec] * 4, out_specs=out_specs, out_shape=out_shape,
        scratch_shapes=[pltpu.VMEM((LANES, S5_SB), F32)],
        compiler_params=_cparams(("parallel",), 48), name="s5_prep",
    )(*ins)


def _s5_kernel(u_ref, g_ref, wz_ref, vt_ref, lp_ref, d_ref, o_ref, u16_ref, z_ref, s_ref, y_ref):
    nc = u16_ref.shape[0]
    kq = S5_T * LANES // 4
    z = None
    for q in range(4):
        for j in range(q * S5_T // 4, (q + 1) * S5_T // 4):
            u16_ref[:, j * LANES:(j + 1) * LANES] = u_ref[pl.ds(j, nc, stride=S5_T), :].astype(BF16)
        zq = jnp.dot(u16_ref[:, q * kq:(q + 1) * kq], wz_ref[q * kq:(q + 1) * kq, :],
                     preferred_element_type=F32)
        z = zq if z is None else z + zq
    z_ref[...] = z

    row = lax.broadcasted_iota(jnp.int32, (SUBLANES, S5_SB), 0)

    def shift(x, sh):
        return jnp.where(row >= sh, pltpu.roll(x, sh, axis=0), 0.0)

    def lp(r0, r1):
        return lp_ref[r0:r1, 0:S5_SB], lp_ref[r0:r1, S5_SB:]

    def scan_rows(r0, carry):
        cr, ci = carry
        xr = z_ref[r0:r0 + SUBLANES, 0:S5_SB]
        xi = z_ref[r0:r0 + SUBLANES, S5_SB:]
        for sh in (1, 2, 4):
            pr, pi = lp(sh, sh + 1)
            sr, si = shift(xr, sh), shift(xi, sh)
            xr, xi = xr + pr * sr - pi * si, xi + pr * si + pi * sr
        pr, pi = lp(0, SUBLANES)
        s_ref[r0:r0 + SUBLANES, 0:S5_SB] = (pr * cr - pi * ci + shift(xr, 1)).astype(s_ref.dtype)
        s_ref[r0:r0 + SUBLANES, S5_SB:] = (pr * ci + pi * cr + shift(xi, 1)).astype(s_ref.dtype)
        pr, pi = lp(SUBLANES, SUBLANES + 1)
        last = SUBLANES - 1
        return (pr * cr - pi * ci + xr[last:last + 1], pr * ci + pi * cr + xi[last:last + 1])

    pairs = S5_T // 2
    for ip in range(pairs):
        kk = (2 * ip + 2) * LANES
        y_ref[:, 2 * ip * LANES:(2 * ip + 2) * LANES] = jnp.dot(
            u16_ref[:, 0:kk], g_ref[(S5_T - 2 - 2 * ip) * LANES:, :], preferred_element_type=F32)

    carry = (jnp.zeros((1, S5_SB), F32), jnp.zeros((1, S5_SB), F32))
    for r0 in range(0, nc, SUBLANES):
        carry = scan_rows(r0, carry)

    sb = s_ref[...].astype(BF16)
    d = d_ref[...]
    for ip in range(pairs):
        acc = y_ref[:, 2 * ip * LANES:(2 * ip + 2) * LANES] + lax.dot_general(
            sb, vt_ref[2 * ip * LANES:(2 * ip + 2) * LANES, :], NT_DIMS, preferred_element_type=F32)
        for e in range(2):
            i = 2 * ip + e
            ui = u_ref[pl.ds(i, nc, stride=S5_T), :]
            o_ref[pl.ds(i, nc, stride=S5_T), :] = acc[:, e * LANES:(e + 1) * LANES] + d * ui


def _s5_apply(a_in, gm, wz, vt, lp, d, layer):
    bsz, seq, width = a_in.shape
    nblk = width // LANES
    nc = seq // S5_T
    kt = S5_T * LANES
    act_spec = pl.BlockSpec((None, seq, LANES), lambda k, b: (b, 0, k))

    def wspec(shape):
        return pl.BlockSpec((None,) + shape, lambda k, b: (layer * nblk + k, 0, 0))

    return pl.pallas_call(
        _s5_kernel, grid=(nblk, bsz),
        in_specs=[act_spec, wspec((kt, 2 * LANES)), wspec((kt, 2 * S5_SB)), wspec((kt, 2 * S5_SB)),
                  wspec((2 * SUBLANES, 2 * S5_SB)), pl.BlockSpec((None, 1, LANES), lambda k, b: (layer, 0, k))],
        out_specs=act_spec,
        out_shape=jax.ShapeDtypeStruct((bsz, seq, width), F32),
        scratch_shapes=[pltpu.VMEM((nc, kt), BF16), pltpu.VMEM((nc, 2 * S5_SB), F32),
                        pltpu.VMEM((nc, 2 * S5_SB), F32), pltpu.VMEM((nc, kt), F32)],
        compiler_params=_cparams(("parallel", "parallel"), 56), name="s5_apply",
    )(a_in, gm, wz, vt, lp, d.astype(F32).reshape(d.shape[0], 1, width))


def _even_tail_kernel(x_ref, ya_ref, sa_ref, hin_ref, sb_ref,
                      wglu_ref, bglu_ref, cw_ref, cb_ref, lng_ref, lnb_ref,
                      wpw_ref, bpw_ref, wout_ref, gpost_ref, o_ref, h_ref, hs_ref, c_ref, *, rows):
    tm, w = c_ref.shape
    nsh = hs_ref.shape[1]

    @pl.when(pl.program_id(1) == 0)
    def _():
        h_ref[0:CONV_HALO, :] = jnp.zeros((CONV_HALO, w), F32)

    h_ref[CONV_HALO:, :] = hin_ref[...].astype(F32)
    for s in range(1, SUBLANES):
        hs_ref[s - 1] = h_ref[s:s + nsh, :]

    off = CONV_HALO - (CONV_K - 1)

    def conv_rows(ri, _):
        r0 = pl.multiple_of(ri * rows, rows)
        for c0 in range(0, w, LANES):
            acc = jnp.broadcast_to(cb_ref[:, c0:c0 + LANES], (rows, LANES))
            for s in range(SUBLANES):
                taps = [o for o in range(off, off + CONV_K) if o % SUBLANES == s]
                src = h_ref if s == 0 else hs_ref.at[s - 1]
                lo, hi = taps[0] - s, taps[-1] - s + rows
                win = src[pl.ds(pl.multiple_of(r0 + lo, SUBLANES), hi - lo), c0:c0 + LANES]
                for o in taps:
                    k = o - off
                    acc = acc + cw_ref[k:k + 1, c0:c0 + LANES] * win[o - s - lo:o - s - lo + rows]
            c_ref[pl.ds(r0, rows), c0:c0 + LANES] = acc
        return 0

    lax.fori_loop(0, tm // rows, conv_rows, 0)
    h_ref[0:CONV_HALO, :] = h_ref[tm:tm + CONV_HALO, :]

    hc = c_ref[...]
    mu = jnp.mean(hc, axis=-1, keepdims=True)
    xc = hc - mu
    hn = xc * lax.rsqrt(jnp.mean(xc * xc, axis=-1, keepdims=True) + EPS) * lng_ref[...] + lnb_ref[...]
    yb = jnp.dot(_silu(hn).astype(BF16), wpw_ref[...], preferred_element_type=F32) + bpw_ref[...]
    yb = yb * sb_ref[...].astype(F32)

    ya = _gelu_tanh(ya_ref[...])
    gate = jnp.dot(ya.astype(BF16), wglu_ref[...], preferred_element_type=F32) + bglu_ref[...]
    ya = ya * _sigmoid(gate) * sa_ref[...].astype(F32)

    y = jnp.dot(ya.astype(BF16), wout_ref[0:w, :], preferred_element_type=F32)
    y = y + jnp.dot(yb.astype(BF16), wout_ref[w:, :], preferred_element_type=F32)
    o_ref[...] = x_ref[...] + y * _rms_scale(y) * gpost_ref[...]


def _even_tail(x, ya, p, i, wglu_b, bglu, cw, cb, lng, lnb, wpw_b, bpw, wout_b, norm_post, tm=512, rows=32):
    bsz, seq, d = x.shape
    w = ya.shape[-1]
    j = i // 2

    def act(col):
        return pl.BlockSpec((None, tm, w), lambda b, t: (b, t, col))

    vec = lambda v: v.astype(F32).reshape(v.shape[0], 1, v.shape[1])
    cw_pad = jnp.pad(cw.astype(F32), ((0, 0), (0, CONV_HALO - CONV_K), (0, 0)))
    consts = [wglu_b, vec(bglu), cw_pad, vec(cb), vec(lng), vec(lnb), wpw_b, vec(bpw), wout_b]
    const_specs = [_layer_spec(c.shape[1:], j) for c in consts] + [_layer_spec((1, d), i)]
    consts.append(vec(norm_post))
    return pl.pallas_call(
        functools.partial(_even_tail_kernel, rows=rows), grid=(bsz, seq // tm),
        in_specs=[act(0), act(0), act(0), act(1), act(2)] + const_specs,
        out_specs=act(0),
        out_shape=jax.ShapeDtypeStruct((bsz, seq, d), F32),
        scratch_shapes=[pltpu.VMEM((CONV_HALO + tm, w), F32),
                        pltpu.VMEM((SUBLANES - 1, CONV_HALO + tm - SUBLANES, w), F32),
                        pltpu.VMEM((tm, w), F32)],
        compiler_params=_cparams(("parallel", "arbitrary"), 56), name="even_tail",
    )(x, ya, p, p, p, *consts)


_GLA_C = GLA_CHUNK
_GLA_TB = 4 * _GLA_C
_R_QD, _R_KD, _R_KU, _R_QX, _R_KX = (i * _GLA_TB for i in range(5))
_R_QHI, _R_KLO, _R_KB0, _R_KB1 = (5 * _GLA_TB + i * 2 * _GLA_C for i in range(4))
_GLA_OPS_ROWS = 7 * _GLA_TB


def _gla_prep(q_b, k_b, lr, wg, bg, ops_ref, dec_ref):
    tb, hk = q_b.shape
    c = _GLA_C
    assert tb == _GLA_TB
    tpc = c // SUBLANES

    gp = jnp.dot(lr, wg, preferred_element_type=F32) + bg
    g = (jnp.minimum(gp, 0.0) - jnp.log(1.0 + jnp.exp(-jnp.abs(gp)))) * (1.0 / GLA_TAU)

    row8 = lax.broadcasted_iota(jnp.int32, (SUBLANES, hk), 0)
    tiles = []
    for i in range(tb // SUBLANES):
        x = g[i * SUBLANES:(i + 1) * SUBLANES]
        for sh in (1, 2, 4):
            x = x + jnp.where(row8 >= sh, pltpu.roll(x, sh, axis=0), 0.0)
        if i % tpc:
            x = x + tiles[-1][SUBLANES - 1:SUBLANES, :]
        tiles.append(x)
    bc = jnp.concatenate(tiles, axis=0)
    bl = [tiles[(a + 1) * tpc - 1][SUBLANES - 1:SUBLANES, :] for a in range(tb // c)]
    bl_rows = jnp.concatenate([jnp.broadcast_to(b, (c, hk)) for b in bl], axis=0)

    q = q_b.astype(F32) * (GLA_HK ** -0.5)
    k = k_b.astype(F32)
    qd = q * jnp.exp(bc)
    kd = k * jnp.exp(-bc)
    ku = k * jnp.exp(bl_rows - bc)
    qd_b, kd_b, ku_b = qd.astype(BF16), kd.astype(BF16), ku.astype(BF16)
    ops_ref[_R_QD:_R_QD + tb] = qd_b
    ops_ref[_R_KD:_R_KD + tb] = kd_b
    ops_ref[_R_KU:_R_KU + tb] = ku_b

    def rows(t, a):
        return t[a * c:(a + 1) * c]

    def put(r0, a, val):
        ops_ref[r0 + a * c:r0 + (a + 1) * c] = val

    e0, e1, e2, e3 = (jnp.exp(b) for b in bl)
    e01 = jnp.exp(bl[0] + bl[1])
    e012 = jnp.exp(bl[0] + bl[1] + bl[2])
    e23 = jnp.exp(bl[2] + bl[3])
    e123 = jnp.exp(bl[1] + bl[2] + bl[3])
    dec_ref[...] = jnp.broadcast_to(jnp.exp(bl[0] + bl[1] + bl[2] + bl[3]), dec_ref.shape)
    put(_R_QX, 0, rows(qd_b, 0))
    put(_R_QX, 1, (rows(qd, 1) * e0).astype(BF16))
    put(_R_QX, 2, (rows(qd, 2) * e01).astype(BF16))
    put(_R_QX, 3, (rows(qd, 3) * e012).astype(BF16))
    put(_R_KX, 0, (rows(ku, 0) * e123).astype(BF16))
    put(_R_KX, 1, (rows(ku, 1) * e23).astype(BF16))
    put(_R_KX, 2, (rows(ku, 2) * e3).astype(BF16))
    put(_R_KX, 3, rows(ku_b, 3))
    put(_R_QHI, 0, rows(qd_b, 2))
    put(_R_QHI, 1, (rows(qd, 3) * e2).astype(BF16))
    put(_R_KLO, 0, (rows(ku, 0) * e1).astype(BF16))
    put(_R_KLO, 1, rows(ku_b, 1))
    put(_R_KB0, 0, rows(ku_b, 0))
    put(_R_KB0, 1, rows(kd_b, 1))
    put(_R_KB1, 0, rows(ku_b, 2))
    put(_R_KB1, 1, rows(kd_b, 3))


def _gla_mix(ops_ref, dec_ref, v, r_b, ng, st_ref):
    c = _GLA_C
    half = 2 * c

    def op(r0, n, a=0):
        return ops_ref[r0 + a * c:r0 + (a + n) * c]

    cross = lax.dot_general(op(_R_QHI, 2), op(_R_KLO, 2), NT_DIMS, preferred_element_type=F32).astype(BF16)
    ri = lax.broadcasted_iota(jnp.int32, (half, half), 0)
    ci = lax.broadcasted_iota(jnp.int32, (half, half), 1)
    causal = ri >= ci

    def half_attn(h):
        top = lax.dot_general(op(_R_QD, 1, 2 * h), op(_R_KD, 2, 2 * h), NT_DIMS, preferred_element_type=F32)
        bot = lax.dot_general(op(_R_QD, 1, 2 * h + 1), op(_R_KB1 if h else _R_KB0, 2), NT_DIMS,
                              preferred_element_type=F32)
        return jnp.where(causal, jnp.concatenate([top, bot], axis=0), 0.0).astype(BF16)

    st = st_ref[...]
    o_top = jnp.dot(half_attn(0), v[0:half], preferred_element_type=F32)
    o_bot = jnp.dot(jnp.concatenate([cross, half_attn(1)], axis=1), v, preferred_element_type=F32)
    o = jnp.concatenate([o_top, o_bot], axis=0)
    o = o + lax.dot_general(op(_R_QX, 4), st.astype(BF16), NT_DIMS, preferred_element_type=F32)
    st_ref[...] = st * dec_ref[0:1, :] + lax.dot_general(v, op(_R_KX, 4), TN_DIMS, preferred_element_type=F32)
    on = o * _rms_scale(o) * ng
    return on * _silu(r_b.astype(F32))


def _gla_kernel(x_ref, q_ref, k_ref, v_ref, r_ref, lr_ref, wg_ref, bg_ref, ng_ref, wout_ref, gpost_ref,
                o_ref, st_ref, ops_ref, dec_ref):
    @pl.when(pl.program_id(1) == 0)
    def _():
        st_ref[...] = jnp.zeros_like(st_ref)

    hk, hv = GLA_HK, GLA_HV
    nheads = st_ref.shape[0]
    ng = ng_ref[...]
    items = [(b, j) for b in range(x_ref.shape[0] // _GLA_TB) for j in range(nheads)]

    def prep(b, j):
        rs, ks = slice(b * _GLA_TB, (b + 1) * _GLA_TB), slice(j * hk, (j + 1) * hk)
        _gla_prep(q_ref[rs, ks], k_ref[rs, ks], lr_ref[rs, :], wg_ref[:, ks], bg_ref[:, ks],
                  ops_ref.at[j], dec_ref.at[j])

    def mix(b, j):
        rs, vs = slice(b * _GLA_TB, (b + 1) * _GLA_TB), slice(j * hv, (j + 1) * hv)
        out = _gla_mix(ops_ref.at[j], dec_ref.at[j], v_ref[rs, vs], r_ref[rs, vs], ng, st_ref.at[j])
        return jnp.dot(out.astype(BF16), wout_ref[vs, :], preferred_element_type=F32)

    prep(*items[0])
    y = None
    for n, (b, j) in enumerate(items):
        if n + 1 < len(items):
            prep(*items[n + 1])
        part = mix(b, j)
        y = part if j == 0 else y + part
        if j == nheads - 1:
            rs = slice(b * _GLA_TB, (b + 1) * _GLA_TB)
            o_ref[rs, :] = x_ref[rs, :] + y * _rms_scale(y) * gpost_ref[...]


def _gla_out(x, p, p_lr, wg, bg, ng, w_out_b, wi, norm_post, gi, tb=2 * _GLA_TB):
    bsz, seq, d = x.shape
    nh = GLA_HEADS
    dk, dv = nh * GLA_HK, nh * GLA_HV

    def act(width, col):
        return pl.BlockSpec((None, tb, width), lambda b, t: (b, t, col))

    in_specs = [act(d, 0), act(dk, 0), act(dk, 1), act(dv, 2 * dk // dv), act(dv, (2 * dk + dv) // dv),
                act(LANES, 0), _const_spec((LANES, dk)), _const_spec((1, dk)), _const_spec((1, GLA_HV)),
                _layer_spec(w_out_b.shape[1:], wi), _layer_spec((1, d), gi)]
    wg_pad = jnp.pad(wg.astype(BF16), ((0, LANES - GLA_LOWRANK), (0, 0)))
    return pl.pallas_call(
        _gla_kernel, grid=(bsz, seq // tb), in_specs=in_specs,
        out_specs=act(d, 0),
        out_shape=jax.ShapeDtypeStruct((bsz, seq, d), F32),
        scratch_shapes=[pltpu.VMEM((nh, GLA_HV, GLA_HK), F32),
                        pltpu.VMEM((nh, _GLA_OPS_ROWS, GLA_HK), BF16),
                        pltpu.VMEM((nh, SUBLANES, GLA_HK), F32)],
        compiler_params=_cparams(("parallel", "arbitrary"), 48), name="gla_out",
    )(x, p, p, p, p, p_lr, wg_pad, bg.reshape(1, dk).astype(F32), ng.reshape(1, GLA_HV).astype(F32),
      w_out_b, norm_post.reshape(norm_post.shape[0], 1, d))


def _even_layer(x, i, norm_pre, norm_post, w_in_b, s5_ops, s5_d, w_glu_b, b_glu, cw, cb, lng, lnb, w_pw_b, b_pw,
                w_out_b):
    bsz, seq, d = x.shape
    j = i // 2
    wa = s5_d.shape[1]
    a_in, gates = _even_in(x.reshape(bsz * seq, d), norm_pre, i, w_in_b, j, wa)
    ya = _s5_apply(a_in.reshape(bsz, seq, wa), *s5_ops, s5_d, j)
    return _even_tail(x, ya, gates.reshape(bsz, seq, -1), i, w_glu_b, b_glu, cw, cb, lng, lnb,
                      w_pw_b, b_pw, w_out_b, norm_post)


def _odd_layer(x, i, norm_pre, norm_post, w_main_b, w_lr_b, wg, bg, ng, w_out_b):
    bsz, seq, d = x.shape
    j = i // 2
    n_main = w_main_b.shape[2] - GLA_LOWRANK
    p, p_lr = _norm_proj(x.reshape(bsz * seq, d), norm_pre, i, [w_main_b, w_lr_b], j,
                         [[n_main], [w_lr_b.shape[2]]], [BF16, BF16])
    return _gla_out(x, p.reshape(bsz, seq, -1), p_lr.reshape(bsz, seq, -1), wg, bg, ng, w_out_b, j, norm_post, i)


def kernel(x, norm_pre, norm_post, ev_w_in, s5_lambda_re, s5_lambda_im, s5_log_dt, s5_b_re, s5_b_im, s5_c_re, s5_c_im, s5_d, s5_w_glu, s5_b_glu, conv_w, conv_b, conv_ln_g, conv_ln_b, conv_w_pw, conv_b_pw, ev_w_out, od_w_in, gla_w_gate_up, gla_b_gate, gla_norm_g, od_w_out):
    depth = norm_pre.shape[0]
    norm_pre, norm_post = norm_pre.astype(F32), norm_post.astype(F32)
    ev_w_in_b, s5_w_glu_b, conv_w_pw_b = ev_w_in.astype(BF16), s5_w_glu.astype(BF16), conv_w_pw.astype(BF16)
    ev_w_out_b, od_w_out_b = ev_w_out.astype(BF16), od_w_out.astype(BF16)
    n_main = od_w_in.shape[2] - GLA_LOWRANK
    od_main_b = od_w_in.astype(BF16)
    od_lr_b = jnp.pad(od_w_in[:, :, n_main:], ((0, 0), (0, 0), (0, LANES - GLA_LOWRANK))).astype(BF16)
    s5_ops = _s5_prep(s5_lambda_re, s5_lambda_im, s5_log_dt, s5_b_re, s5_b_im, s5_c_re, s5_c_im)
    for i in range(depth):
        j = i // 2
        if i % 2 == 0:
            x = _even_layer(x, i, norm_pre, norm_post, ev_w_in_b, s5_ops, s5_d, s5_w_glu_b,
                            s5_b_glu, conv_w, conv_b, conv_ln_g, conv_ln_b, conv_w_pw_b, conv_b_pw, ev_w_out_b)
        else:
            x = _odd_layer(x, i, norm_pre, norm_post, od_main_b, od_lr_b, gla_w_gate_up[j], gla_b_gate[j],
                           gla_norm_g[j], od_w_out_b)
    return x
```

```python
import functools

import jax
import jax.numpy as jnp
from jax import lax
from jax.experimental import pallas as pl
from jax.experimental.pallas import tpu as pltpu

F32 = jnp.float32
BF16 = jnp.bfloat16

EPS = 1e-6
LANES = 128
SUBLANES = 8

S5_GROUP = 16
S5_STATE = 64
S5_T = 16
S5_GPB = LANES // S5_GROUP
S5_SB = S5_GPB * S5_STATE
CONV_K = 31
CONV_HALO = 32
GLA_HEADS = 4
GLA_HK = 256
GLA_HV = 512
GLA_LOWRANK = 16
GLA_TAU = 16.0
GLA_CHUNK = 64

NT_DIMS = (((1,), (1,)), ((), ()))
TN_DIMS = (((0,), (0,)), ((), ()))


def _cparams(sem, vmem_mb):
    return pltpu.CompilerParams(dimension_semantics=sem, vmem_limit_bytes=vmem_mb * 1024 * 1024)


def _const_spec(shape):
    nd = len(shape)
    return pl.BlockSpec(shape, lambda *_: (0,) * nd, pipeline_mode=pl.Buffered(1))


def _rms_scale(x):
    return lax.rsqrt(jnp.mean(x * x, axis=-1, keepdims=True) + EPS)


def _sigmoid(x):
    return 0.5 * jnp.tanh(0.5 * x) + 0.5


def _silu(x):
    h = 0.5 * x
    return h + h * jnp.tanh(h)


def _gelu_tanh(x):
    c = 0.7978845608028654
    h = 0.5 * x
    return h + h * jnp.tanh(x * (c + (c * 0.044715) * (x * x)))


def _layer_spec(shape, layer):
    nd = len(shape)
    return pl.BlockSpec((None,) + tuple(shape), lambda *_: (layer,) + (0,) * nd, pipeline_mode=pl.Buffered(1))


def _norm_proj_kernel(x_ref, g_ref, *refs, n_w, splits, n_chunk):
    w_refs, o_refs = refs[:n_w], refs[n_w:]
    x = x_ref[...]
    u = (x * _rms_scale(x) * g_ref[...]).astype(BF16)
    outs = iter(o_refs)
    for w_ref, widths in zip(w_refs, splits):
        base = 0
        for width in widths:
            o_ref = next(outs)
            for n0 in range(0, width, n_chunk):
                n1 = min(n0 + n_chunk, width)
                o_ref[:, n0:n1] = jnp.dot(u, w_ref[:, base + n0:base + n1],
                                          preferred_element_type=F32).astype(o_ref.dtype)
            base += width


def _norm_proj(x2, g_stack, gi, w_stacks, wi, splits, out_dtypes, tm=512, n_chunk=512):
    m, d = x2.shape
    widths = [wd for ws in splits for wd in ws]
    in_specs = [pl.BlockSpec((tm, d), lambda i: (i, 0)), _layer_spec((1, d), gi)]
    in_specs += [_layer_spec((d, sum(ws)), wi) for ws in splits]
    out_specs = [pl.BlockSpec((tm, wd), lambda i: (i, 0)) for wd in widths]
    out_shape = [jax.ShapeDtypeStruct((m, wd), dt) for wd, dt in zip(widths, out_dtypes)]
    return pl.pallas_call(
        functools.partial(_norm_proj_kernel, n_w=len(w_stacks), splits=splits, n_chunk=n_chunk),
        grid=(m // tm,), in_specs=in_specs, out_specs=out_specs, out_shape=out_shape,
        compiler_params=_cparams(("parallel",), 40), name="norm_proj",
    )(x2, g_stack.reshape(g_stack.shape[0], 1, d), *w_stacks)


def _even_in_kernel(x_ref, g_ref, w_ref, a_ref, gates_ref, *, rows, n_chunk):
    tm, wa = a_ref.shape
    for r0 in range(0, tm, rows):
        rs = slice(r0, r0 + rows)
        x = x_ref[rs, :]
        u = (x * _rms_scale(x) * g_ref[...]).astype(BF16)

        def proj(part, n0):
            c0 = part * wa + n0
            return jnp.dot(u, w_ref[:, c0:c0 + n_chunk], preferred_element_type=F32)

        for n0 in range(0, wa, n_chunk):
            cols = slice(n0, n0 + n_chunk)
            a_ref[rs, cols] = proj(0, n0)
            gates_ref[rs, cols] = _silu(proj(1, n0)).astype(gates_ref.dtype)
            gates_ref[rs, wa + n0:wa + n0 + n_chunk] = (
                proj(2, n0) * _sigmoid(proj(3, n0))).astype(gates_ref.dtype)
            gates_ref[rs, 2 * wa + n0:2 * wa + n0 + n_chunk] = _silu(proj(4, n0)).astype(gates_ref.dtype)


def _even_in(x2, g_stack, gi, w_stack, wi, wa, tm=1024, rows=512, n_chunk=512):
    m, d = x2.shape
    assert w_stack.shape[2] == 5 * wa and wa % n_chunk == 0
    return pl.pallas_call(
        functools.partial(_even_in_kernel, rows=rows, n_chunk=n_chunk), grid=(m // tm,),
        in_specs=[pl.BlockSpec((tm, d), lambda i: (i, 0)), _layer_spec((1, d), gi),
                  _layer_spec(w_stack.shape[1:], wi)],
        out_specs=[pl.BlockSpec((tm, wa), lambda i: (i, 0)), pl.BlockSpec((tm, 3 * wa), lambda i: (i, 0))],
        out_shape=[jax.ShapeDtypeStruct((m, wa), F32), jax.ShapeDtypeStruct((m, 3 * wa), BF16)],
        compiler_params=_cparams(("parallel",), 56), name="even_in",
    )(x2, g_stack.reshape(g_stack.shape[0], 1, d), w_stack)


def _s5_prep_kernel(lre_ref, lim_ref, dt_ref, btr_ref, bti_ref, cr_ref, ci_ref,
                    g_ref, wz_ref, vt_ref, lp_ref, own_ref):
    lre, lim, dt = lre_ref[...], lim_ref[...], jnp.exp(dt_ref[...])

    def lam_pow(m):
        mag = jnp.exp(lre * dt * float(m))
        ang = lim * dt * float(m)
        return mag * jnp.cos(ang), mag * jnp.sin(ang)

    def per_channel(t):
        return jnp.concatenate([jnp.broadcast_to(t[g:g + 1], (S5_GROUP, S5_STATE)) for g in range(S5_GPB)],
                               axis=0)

    def lane_tile(t):
        t2 = jnp.concatenate([t, t], axis=1)
        return jnp.concatenate([t2] * (S5_SB // LANES), axis=1)

    pows_g = [lam_pow(m) for m in range(S5_T + 1)]
    pows = [(per_channel(pr), per_channel(pi)) for pr, pi in pows_g]
    nr, ni = pows_g[1][0] - 1.0, pows_g[1][1]
    den = lre * lre + lim * lim
    cfr = per_channel((nr * lre + ni * lim) / den)
    cfi = per_channel((ni * lre - nr * lim) / den)
    btr, bti = btr_ref[...], bti_ref[...]
    bbr = cfr * btr - cfi * bti
    bbi = cfr * bti + cfi * btr
    cr, ci = cr_ref[...], ci_ref[...]

    row_g = lax.broadcasted_iota(jnp.int32, (LANES, S5_SB), 0) // S5_GROUP
    lane_g = lax.broadcasted_iota(jnp.int32, (LANES, S5_SB), 1) // S5_STATE
    own_ref[...] = jnp.where(row_g == lane_g, 1.0, 0.0)

    def place(re, im):
        def own(t):
            return jnp.where(own_ref[...] != 0.0, lane_tile(t), 0.0)
        return jnp.concatenate([own(re), own(im)], axis=1).astype(BF16)

    def hi_lo(re, im):
        re_hi, im_hi = re.astype(BF16).astype(F32), im.astype(BF16).astype(F32)
        return place(re_hi, im_hi), place(re - re_hi, im - im_hi)

    def dot_nt(a, b):
        return lax.dot_general(a, b, NT_DIMS, preferred_element_type=F32)

    for j in range(S5_T - 1):
        pr, pi = pows[S5_T - 1 - j]
        wz_ref[j * LANES:(j + 1) * LANES, :] = place(pr * bbr - pi * bbi, pr * bbi + pi * bbr)
    bb_hi, bb_lo = hi_lo(bbr, bbi)
    wz_ref[(S5_T - 1) * LANES:, :] = bb_hi
    kts = []
    for tau in range(S5_T + 1):
        pr, pi = pows[tau]
        ct_hi, ct_lo = hi_lo(cr * pr - ci * pi, -(cr * pi + ci * pr))
        if tau >= 1:
            vt_ref[(tau - 1) * LANES:tau * LANES, :] = ct_hi
        if tau < S5_T:
            kts.append(dot_nt(bb_hi, ct_hi) + dot_nt(bb_hi, ct_lo) + dot_nt(bb_lo, ct_hi))
    zero = jnp.zeros((LANES, LANES), F32)
    for r in range(S5_T):
        for e in range(2):
            tau = S5_T - 2 - r + e
            kt = kts[tau] if tau >= 0 else zero
            g_ref[r * LANES:(r + 1) * LANES, e * LANES:(e + 1) * LANES] = kt.astype(g_ref.dtype)
    own_row = (lax.broadcasted_iota(jnp.int32, (S5_GPB, S5_SB), 0)
               == lax.broadcasted_iota(jnp.int32, (S5_GPB, S5_SB), 1) // S5_STATE)

    def as_row(t):
        return jnp.sum(jnp.where(own_row, lane_tile(t), 0.0), axis=0, keepdims=True)

    lp_ref[...] = jnp.zeros_like(lp_ref)
    for r in range(SUBLANES + 1):
        pr, pi = lam_pow(S5_T * r)
        lp_ref[r:r + 1, 0:S5_SB] = as_row(pr)
        lp_ref[r:r + 1, S5_SB:] = as_row(pi)


def _s5_prep(lam_re, lam_im, log_dt, b_re, b_im, c_re, c_im):
    nl, ng = lam_re.shape[:2]
    nblk = nl * ng // S5_GPB

    def groups(t):
        return t.astype(F32).reshape(nblk, S5_GPB, S5_STATE)

    def mats(t):
        return t.astype(F32).reshape(nblk, LANES, S5_STATE)

    dt_g = jnp.broadcast_to(log_dt.astype(F32)[:, :, None], (nl, ng, S5_STATE))
    ins = [groups(lam_re), groups(lam_im), groups(dt_g),
           mats(jnp.swapaxes(b_re, 2, 3)), mats(jnp.swapaxes(b_im, 2, 3)), mats(c_re), mats(c_im)]
    grp_spec = pl.BlockSpec((None, S5_GPB, S5_STATE), lambda k: (k, 0, 0))
    mat_spec = pl.BlockSpec((None, LANES, S5_STATE), lambda k: (k, 0, 0))
    kt = S5_T * LANES
    out_shape = [jax.ShapeDtypeStruct((nblk, kt, 2 * LANES), BF16),
                 jax.ShapeDtypeStruct((nblk, kt, 2 * S5_SB), BF16),
                 jax.ShapeDtypeStruct((nblk, kt, 2 * S5_SB), BF16),
                 jax.ShapeDtypeStruct((nblk, 2 * SUBLANES, 2 * S5_SB), F32)]
    out_specs = [pl.BlockSpec((None,) + s.shape[1:], lambda k: (k, 0, 0)) for s in out_shape]
    return pl.pallas_call(
        _s5_prep_kernel, grid=(nblk,),
        in_specs=[grp_spec] * 3 + [mat_spec] * 4, out_specs=out_specs, out_shape=out_shape,
        scratch_shapes=[pltpu.VMEM((LANES, S5_SB), F32)],
        compiler_params=_cparams(("parallel",), 32), name="s5_prep",
    )(*ins)


def _s5_kernel(u_ref, g_ref, wz_ref, vt_ref, lp_ref, d_ref, o_ref, u16_ref, z_ref, s_ref, y_ref):
    nc = u16_ref.shape[0]
    kq = S5_T * LANES // 4
    z = None
    for q in range(4):
        for j in range(q * S5_T // 4, (q + 1) * S5_T // 4):
            u16_ref[:, j * LANES:(j + 1) * LANES] = u_ref[pl.ds(j, nc, stride=S5_T), :].astype(BF16)
        zq = jnp.dot(u16_ref[:, q * kq:(q + 1) * kq], wz_ref[q * kq:(q + 1) * kq, :],
                     preferred_element_type=F32)
        z = zq if z is None else z + zq
    z_ref[...] = z

    row = lax.broadcasted_iota(jnp.int32, (SUBLANES, S5_SB), 0)

    def shift(x, sh):
        return jnp.where(row >= sh, pltpu.roll(x, sh, axis=0), 0.0)

    def lp(r0, r1):
        return lp_ref[r0:r1, 0:S5_SB], lp_ref[r0:r1, S5_SB:]

    def scan_rows(r0, carry):
        cr, ci = carry
        xr = z_ref[r0:r0 + SUBLANES, 0:S5_SB]
        xi = z_ref[r0:r0 + SUBLANES, S5_SB:]
        for sh in (1, 2, 4):
            pr, pi = lp(sh, sh + 1)
            sr, si = shift(xr, sh), shift(xi, sh)
            xr, xi = xr + pr * sr - pi * si, xi + pr * si + pi * sr
        pr, pi = lp(0, SUBLANES)
        s_ref[r0:r0 + SUBLANES, 0:S5_SB] = (pr * cr - pi * ci + shift(xr, 1)).astype(s_ref.dtype)
        s_ref[r0:r0 + SUBLANES, S5_SB:] = (pr * ci + pi * cr + shift(xi, 1)).astype(s_ref.dtype)
        pr, pi = lp(SUBLANES, SUBLANES + 1)
        last = SUBLANES - 1
        return (pr * cr - pi * ci + xr[last:last + 1], pr * ci + pi * cr + xi[last:last + 1])

    pairs = S5_T // 2
    for ip in range(pairs):
        kk = (2 * ip + 2) * LANES
        y_ref[:, 2 * ip * LANES:(2 * ip + 2) * LANES] = jnp.dot(
            u16_ref[:, 0:kk], g_ref[(S5_T - 2 - 2 * ip) * LANES:, :], preferred_element_type=F32)

    carry = (jnp.zeros((1, S5_SB), F32), jnp.zeros((1, S5_SB), F32))
    for r0 in range(0, nc, SUBLANES):
        carry = scan_rows(r0, carry)

    sb = s_ref[...].astype(BF16)
    d = d_ref[...]
    for ip in range(pairs):
        acc = y_ref[:, 2 * ip * LANES:(2 * ip + 2) * LANES] + lax.dot_general(
            sb, vt_ref[2 * ip * LANES:(2 * ip + 2) * LANES, :], NT_DIMS, preferred_element_type=F32)
        for e in range(2):
            i = 2 * ip + e
            ui = u_ref[pl.ds(i, nc, stride=S5_T), :]
            o_ref[pl.ds(i, nc, stride=S5_T), :] = acc[:, e * LANES:(e + 1) * LANES] + d * ui


def _s5_apply(a_in, gm, wz, vt, lp, d, layer):
    bsz, seq, width = a_in.shape
    nblk = width // LANES
    nc = seq // S5_T
    kt = S5_T * LANES
    act_spec = pl.BlockSpec((None, seq, LANES), lambda k, b: (b, 0, k))

    def wspec(shape):
        return pl.BlockSpec((None,) + shape, lambda k, b: (layer * nblk + k, 0, 0))

    return pl.pallas_call(
        _s5_kernel, grid=(nblk, bsz),
        in_specs=[act_spec, wspec((kt, 2 * LANES)), wspec((kt, 2 * S5_SB)), wspec((kt, 2 * S5_SB)),
                  wspec((2 * SUBLANES, 2 * S5_SB)), pl.BlockSpec((None, 1, LANES), lambda k, b: (layer, 0, k))],
        out_specs=act_spec,
        out_shape=jax.ShapeDtypeStruct((bsz, seq, width), F32),
        scratch_shapes=[pltpu.VMEM((nc, kt), BF16), pltpu.VMEM((nc, 2 * S5_SB), F32),
                        pltpu.VMEM((nc, 2 * S5_SB), F32), pltpu.VMEM((nc, kt), F32)],
        compiler_params=_cparams(("parallel", "parallel"), 40), name="s5_apply",
    )(a_in, gm, wz, vt, lp, d.astype(F32).reshape(d.shape[0], 1, width))


def _even_tail_kernel(x_ref, ya_ref, sa_ref, hin_ref, sb_ref,
                      wglu_ref, bglu_ref, cw_ref, cb_ref, lng_ref, lnb_ref,
                      wpw_ref, bpw_ref, wout_ref, gpost_ref, o_ref, h_ref, hs_ref, c_ref, *, rows):
    tm, w = c_ref.shape
    nsh = hs_ref.shape[1]

    @pl.when(pl.program_id(1) == 0)
    def _():
        h_ref[0:CONV_HALO, :] = jnp.zeros((CONV_HALO, w), F32)

    h_ref[CONV_HALO:, :] = hin_ref[...].astype(F32)
    for s in range(1, SUBLANES):
        hs_ref[s - 1] = h_ref[s:s + nsh, :]

    off = CONV_HALO - (CONV_K - 1)

    def conv_rows(ri, _):
        r0 = pl.multiple_of(ri * rows, rows)
        for c0 in range(0, w, LANES):
            acc = jnp.broadcast_to(cb_ref[:, c0:c0 + LANES], (rows, LANES))
            for s in range(SUBLANES):
                taps = [o for o in range(off, off + CONV_K) if o % SUBLANES == s]
                src = h_ref if s == 0 else hs_ref.at[s - 1]
                lo, hi = taps[0] - s, taps[-1] - s + rows
                win = src[pl.ds(pl.multiple_of(r0 + lo, SUBLANES), hi - lo), c0:c0 + LANES]
                for o in taps:
                    k = o - off
                    acc = acc + cw_ref[k:k + 1, c0:c0 + LANES] * win[o - s - lo:o - s - lo + rows]
            c_ref[pl.ds(r0, rows), c0:c0 + LANES] = acc
        return 0

    lax.fori_loop(0, tm // rows, conv_rows, 0)
    h_ref[0:CONV_HALO, :] = h_ref[tm:tm + CONV_HALO, :]

    hc = c_ref[...]
    mu = jnp.mean(hc, axis=-1, keepdims=True)
    xc = hc - mu
    hn = xc * lax.rsqrt(jnp.mean(xc * xc, axis=-1, keepdims=True) + EPS) * lng_ref[...] + lnb_ref[...]
    yb = jnp.dot(_silu(hn).astype(BF16), wpw_ref[...], preferred_element_type=F32) + bpw_ref[...]
    yb = yb * sb_ref[...].astype(F32)

    ya = _gelu_tanh(ya_ref[...])
    gate = jnp.dot(ya.astype(BF16), wglu_ref[...], preferred_element_type=F32) + bglu_ref[...]
    ya = ya * _sigmoid(gate) * sa_ref[...].astype(F32)

    y = jnp.dot(ya.astype(BF16), wout_ref[0:w, :], preferred_element_type=F32)
    y = y + jnp.dot(yb.astype(BF16), wout_ref[w:, :], preferred_element_type=F32)
    o_ref[...] = x_ref[...] + y * _rms_scale(y) * gpost_ref[...]


def _even_tail(x, ya, p, i, wglu_b, bglu, cw, cb, lng, lnb, wpw_b, bpw, wout_b, norm_post, tm=512, rows=32):
    bsz, seq, d = x.shape
    w = ya.shape[-1]
    j = i // 2

    def act(col):
        return pl.BlockSpec((None, tm, w), lambda b, t: (b, t, col))

    vec = lambda v: v.astype(F32).reshape(v.shape[0], 1, v.shape[1])
    cw_pad = jnp.pad(cw.astype(F32), ((0, 0), (0, CONV_HALO - CONV_K), (0, 0)))
    consts = [wglu_b, vec(bglu), cw_pad, vec(cb), vec(lng), vec(lnb), wpw_b, vec(bpw), wout_b]
    const_specs = [_layer_spec(c.shape[1:], j) for c in consts] + [_layer_spec((1, d), i)]
    consts.append(vec(norm_post))
    return pl.pallas_call(
        functools.partial(_even_tail_kernel, rows=rows), grid=(bsz, seq // tm),
        in_specs=[act(0), act(0), act(0), act(1), act(2)] + const_specs,
        out_specs=act(0),
        out_shape=jax.ShapeDtypeStruct((bsz, seq, d), F32),
        scratch_shapes=[pltpu.VMEM((CONV_HALO + tm, w), F32),
                        pltpu.VMEM((SUBLANES - 1, CONV_HALO + tm - SUBLANES, w), F32),
                        pltpu.VMEM((tm, w), F32)],
        compiler_params=_cparams(("parallel", "arbitrary"), 56), name="even_tail",
    )(x, ya, p, p, p, *consts)


_GLA_C = GLA_CHUNK
_GLA_TB = 4 * _GLA_C
_R_QD, _R_KD, _R_KU, _R_QX, _R_KX = (i * _GLA_TB for i in range(5))
_R_QHI, _R_KLO, _R_KB0, _R_KB1 = (5 * _GLA_TB + i * 2 * _GLA_C for i in range(4))
_GLA_OPS_ROWS = 7 * _GLA_TB


def _gla_prep(q_b, k_b, lr, wg, bg, ops_ref, dec_ref):
    tb, hk = q_b.shape
    c = _GLA_C
    assert tb == _GLA_TB
    tpc = c // SUBLANES

    gp = jnp.dot(lr, wg, preferred_element_type=F32) + bg
    g = (jnp.minimum(gp, 0.0) - jnp.log(1.0 + jnp.exp(-jnp.abs(gp)))) * (1.0 / GLA_TAU)

    row8 = lax.broadcasted_iota(jnp.int32, (SUBLANES, hk), 0)
    tiles = []
    for i in range(tb // SUBLANES):
        x = g[i * SUBLANES:(i + 1) * SUBLANES]
        for sh in (1, 2, 4):
            x = x + jnp.where(row8 >= sh, pltpu.roll(x, sh, axis=0), 0.0)
        if i % tpc:
            x = x + tiles[-1][SUBLANES - 1:SUBLANES, :]
        tiles.append(x)
    bc = jnp.concatenate(tiles, axis=0)
    bl = [tiles[(a + 1) * tpc - 1][SUBLANES - 1:SUBLANES, :] for a in range(tb // c)]
    bl_rows = jnp.concatenate([jnp.broadcast_to(b, (c, hk)) for b in bl], axis=0)

    q = q_b.astype(F32) * (GLA_HK ** -0.5)
    k = k_b.astype(F32)
    qd = q * jnp.exp(bc)
    kd = k * jnp.exp(-bc)
    ku = k * jnp.exp(bl_rows - bc)
    qd_b, kd_b, ku_b = qd.astype(BF16), kd.astype(BF16), ku.astype(BF16)
    ops_ref[_R_QD:_R_QD + tb] = qd_b
    ops_ref[_R_KD:_R_KD + tb] = kd_b
    ops_ref[_R_KU:_R_KU + tb] = ku_b

    def rows(t, a):
        return t[a * c:(a + 1) * c]

    def put(r0, a, val):
        ops_ref[r0 + a * c:r0 + (a + 1) * c] = val

    e0, e1, e2, e3 = (jnp.exp(b) for b in bl)
    e01 = jnp.exp(bl[0] + bl[1])
    e012 = jnp.exp(bl[0] + bl[1] + bl[2])
    e23 = jnp.exp(bl[2] + bl[3])
    e123 = jnp.exp(bl[1] + bl[2] + bl[3])
    dec_ref[...] = jnp.broadcast_to(jnp.exp(bl[0] + bl[1] + bl[2] + bl[3]), dec_ref.shape)
    put(_R_QX, 0, rows(qd_b, 0))
    put(_R_QX, 1, (rows(qd, 1) * e0).astype(BF16))
    put(_R_QX, 2, (rows(qd, 2) * e01).astype(BF16))
    put(_R_QX, 3, (rows(qd, 3) * e012).astype(BF16))
    put(_R_KX, 0, (rows(ku, 0) * e123).astype(BF16))
    put(_R_KX, 1, (rows(ku, 1) * e23).astype(BF16))
    put(_R_KX, 2, (rows(ku, 2) * e3).astype(BF16))
    put(_R_KX, 3, rows(ku_b, 3))
    put(_R_QHI, 0, rows(qd_b, 2))
    put(_R_QHI, 1, (rows(qd, 3) * e2).astype(BF16))
    put(_R_KLO, 0, (rows(ku, 0) * e1).astype(BF16))
    put(_R_KLO, 1, rows(ku_b, 1))
    put(_R_KB0, 0, rows(ku_b, 0))
    put(_R_KB0, 1, rows(kd_b, 1))
    put(_R_KB1, 0, rows(ku_b, 2))
    put(_R_KB1, 1, rows(kd_b, 3))


def _gla_mix(ops_ref, dec_ref, v, r_b, ng, st_ref):
    c = _GLA_C
    half = 2 * c

    def op(r0, n, a=0):
        return ops_ref[r0 + a * c:r0 + (a + n) * c]

    cross = lax.dot_general(op(_R_QHI, 2), op(_R_KLO, 2), NT_DIMS, preferred_element_type=F32).astype(BF16)
    ri = lax.broadcasted_iota(jnp.int32, (half, half), 0)
    ci = lax.broadcasted_iota(jnp.int32, (half, half), 1)
    causal = ri >= ci

    def half_attn(h):
        top = lax.dot_general(op(_R_QD, 1, 2 * h), op(_R_KD, 2, 2 * h), NT_DIMS, preferred_element_type=F32)
        bot = lax.dot_general(op(_R_QD, 1, 2 * h + 1), op(_R_KB1 if h else _R_KB0, 2), NT_DIMS,
                              preferred_element_type=F32)
        return jnp.where(causal, jnp.concatenate([top, bot], axis=0), 0.0).astype(BF16)

    st = st_ref[...]
    o_top = jnp.dot(half_attn(0), v[0:half], preferred_element_type=F32)
    o_bot = jnp.dot(jnp.concatenate([cross, half_attn(1)], axis=1), v, preferred_element_type=F32)
    o = jnp.concatenate([o_top, o_bot], axis=0)
    o = o + lax.dot_general(op(_R_QX, 4), st.astype(BF16), NT_DIMS, preferred_element_type=F32)
    st_ref[...] = st * dec_ref[0:1, :] + lax.dot_general(v, op(_R_KX, 4), TN_DIMS, preferred_element_type=F32)
    on = o * _rms_scale(o) * ng
    return on * _silu(r_b.astype(F32))


def _gla_kernel(x_ref, q_ref, k_ref, v_ref, r_ref, lr_ref, wg_ref, bg_ref, ng_ref, wout_ref, gpost_ref,
                o_ref, st_ref, ops_ref, dec_ref):
    @pl.when(pl.program_id(1) == 0)
    def _():
        st_ref[...] = jnp.zeros_like(st_ref)

    hk, hv = GLA_HK, GLA_HV
    nheads = st_ref.shape[0]
    ng = ng_ref[...]
    items = [(b, j) for b in range(x_ref.shape[0] // _GLA_TB) for j in range(nheads)]

    def prep(b, j):
        rs, ks = slice(b * _GLA_TB, (b + 1) * _GLA_TB), slice(j * hk, (j + 1) * hk)
        _gla_prep(q_ref[rs, ks], k_ref[rs, ks], lr_ref[rs, :], wg_ref[:, ks], bg_ref[:, ks],
                  ops_ref.at[j], dec_ref.at[j])

    def mix(b, j):
        rs, vs = slice(b * _GLA_TB, (b + 1) * _GLA_TB), slice(j * hv, (j + 1) * hv)
        out = _gla_mix(ops_ref.at[j], dec_ref.at[j], v_ref[rs, vs], r_ref[rs, vs], ng, st_ref.at[j])
        return jnp.dot(out.astype(BF16), wout_ref[vs, :], preferred_element_type=F32)

    prep(*items[0])
    y = None
    for n, (b, j) in enumerate(items):
        if n + 1 < len(items):
            prep(*items[n + 1])
        part = mix(b, j)
        y = part if j == 0 else y + part
        if j == nheads - 1:
            rs = slice(b * _GLA_TB, (b + 1) * _GLA_TB)
            o_ref[rs, :] = x_ref[rs, :] + y * _rms_scale(y) * gpost_ref[...]


def _gla_out(x, p, p_lr, wg, bg, ng, w_out_b, wi, norm_post, gi, tb=2 * _GLA_TB):
    bsz, seq, d = x.shape
    nh = GLA_HEADS
    dk, dv = nh * GLA_HK, nh * GLA_HV

    def act(width, col):
        return pl.BlockSpec((None, tb, width), lambda b, t: (b, t, col))

    in_specs = [act(d, 0), act(dk, 0), act(dk, 1), act(dv, 2 * dk // dv), act(dv, (2 * dk + dv) // dv),
                act(LANES, 0), _const_spec((LANES, dk)), _const_spec((1, dk)), _const_spec((1, GLA_HV)),
                _layer_spec(w_out_b.shape[1:], wi), _layer_spec((1, d), gi)]
    wg_pad = jnp.pad(wg.astype(BF16), ((0, LANES - GLA_LOWRANK), (0, 0)))
    return pl.pallas_call(
        _gla_kernel, grid=(bsz, seq // tb), in_specs=in_specs,
        out_specs=act(d, 0),
        out_shape=jax.ShapeDtypeStruct((bsz, seq, d), F32),
        scratch_shapes=[pltpu.VMEM((nh, GLA_HV, GLA_HK), F32),
                        pltpu.VMEM((nh, _GLA_OPS_ROWS, GLA_HK), BF16),
                        pltpu.VMEM((nh, SUBLANES, GLA_HK), F32)],
        compiler_params=_cparams(("parallel", "arbitrary"), 40), name="gla_out",
    )(x, p, p, p, p, p_lr, wg_pad, bg.reshape(1, dk).astype(F32), ng.reshape(1, GLA_HV).astype(F32),
      w_out_b, norm_post.reshape(norm_post.shape[0], 1, d))


def _even_layer(x, i, norm_pre, norm_post, w_in_b, s5_ops, s5_d, w_glu_b, b_glu, cw, cb, lng, lnb, w_pw_b, b_pw,
                w_out_b):
    bsz, seq, d = x.shape
    j = i // 2
    wa = s5_d.shape[1]
    a_in, gates = _even_in(x.reshape(bsz * seq, d), norm_pre, i, w_in_b, j, wa)
    ya = _s5_apply(a_in.reshape(bsz, seq, wa), *s5_ops, s5_d, j)
    return _even_tail(x, ya, gates.reshape(bsz, seq, -1), i, w_glu_b, b_glu, cw, cb, lng, lnb,
                      w_pw_b, b_pw, w_out_b, norm_post)


def _odd_layer(x, i, norm_pre, norm_post, w_main_b, w_lr_b, wg, bg, ng, w_out_b):
    bsz, seq, d = x.shape
    j = i // 2
    n_main = w_main_b.shape[2] - GLA_LOWRANK
    p, p_lr = _norm_proj(x.reshape(bsz * seq, d), norm_pre, i, [w_main_b, w_lr_b], j,
                         [[n_main], [w_lr_b.shape[2]]], [BF16, BF16])
    return _gla_out(x, p.reshape(bsz, seq, -1), p_lr.reshape(bsz, seq, -1), wg, bg, ng, w_out_b, j, norm_post, i)


def kernel(x, norm_pre, norm_post, ev_w_in, s5_lambda_re, s5_lambda_im, s5_log_dt, s5_b_re, s5_b_im, s5_c_re, s5_c_im, s5_d, s5_w_glu, s5_b_glu, conv_w, conv_b, conv_ln_g, conv_ln_b, conv_w_pw, conv_b_pw, ev_w_out, od_w_in, gla_w_gate_up, gla_b_gate, gla_norm_g, od_w_out):
    depth = norm_pre.shape[0]
    norm_pre, norm_post = norm_pre.astype(F32), norm_post.astype(F32)
    ev_w_in_b, s5_w_glu_b, conv_w_pw_b = ev_w_in.astype(BF16), s5_w_glu.astype(BF16), conv_w_pw.astype(BF16)
    ev_w_out_b, od_w_out_b = ev_w_out.astype(BF16), od_w_out.astype(BF16)
    n_main = od_w_in.shape[2] - GLA_LOWRANK
    od_main_b = od_w_in.astype(BF16)
    od_lr_b = jnp.pad(od_w_in[:, :, n_main:], ((0, 0), (0, 0), (0, LANES - GLA_LOWRANK))).astype(BF16)
    s5_ops = _s5_prep(s5_lambda_re, s5_lambda_im, s5_log_dt, s5_b_re, s5_b_im, s5_c_re, s5_c_im)
    for i in range(depth):
        j = i // 2
        if i % 2 == 0:
            x = _even_layer(x, i, norm_pre, norm_post, ev_w_in_b, s5_ops, s5_d, s5_w_glu_b,
                            s5_b_glu, conv_w, conv_b, conv_ln_g, conv_ln_b, conv_w_pw_b, conv_b_pw, ev_w_out_b)
        else:
            x = _odd_layer(x, i, norm_pre, norm_post, od_main_b, od_lr_b, gla_w_gate_up[j], gla_b_gate[j],
                           gla_norm_g[j], od_w_out_b)
    return x
```
